```python
import jax, jax.numpy as jnp
from jax import lax
import numpy as np

D_MODEL = 1024
BATCH = 2
SEQ = 8192
DEPTH = 4
DEC_BATCH = 32
DEC_SEQ = 4
PAST_LEN = 8192
PAGE_SIZE = 128

N_MIXERS = 2
N_ATTN = (DEPTH + 1) // 2
N_CONV = DEPTH // 2
N_HEADS = 16
HEAD_DIM = D_MODEL // N_HEADS
N_KV = 4
GROUP = N_HEADS // N_KV
KV_W = N_KV * HEAD_DIM
Q_W = N_HEADS * HEAD_DIM
IN_COLS = Q_W + 6 * KV_W + 3 * N_HEADS
CMP_LEN = 32
CMP_STRIDE = 16
CMP_RATIO = CMP_LEN // CMP_STRIDE
CMP_HIDDEN = HEAD_DIM
SEL_BLOCK = 64
SEL_TOP = 16
WINDOW = 512
Q_BLOCK = 128
ROPE_THETA = 10000.0
CONV_WIDTH = 31
D_INNER = D_MODEL
D_FF = (8 * D_MODEL // 3 + 127) // 128 * 128
EPS = 1e-6
NEG = -1e30

kernel_name = 'nsa_conformer_macaron_hybrid_step'


def rms_norm(x, g):
    xf = x.astype(jnp.float32)
    y = xf * lax.rsqrt(jnp.mean(xf * xf, axis=-1, keepdims=True) + EPS)
    return (y * g.astype(jnp.float32)).astype(x.dtype)


def layer_norm(x, g, b):
    xf = x.astype(jnp.float32)
    xc = xf - jnp.mean(xf, axis=-1, keepdims=True)
    y = xc * lax.rsqrt(jnp.mean(xc * xc, axis=-1, keepdims=True) + EPS)
    return (y * g.astype(jnp.float32) + b.astype(jnp.float32)).astype(x.dtype)


def rope(x, pos):
    half = HEAD_DIM // 2
    inv = ROPE_THETA ** (-jnp.arange(half, dtype=jnp.float32) / half)
    ang = pos.astype(jnp.float32)[:, None] * inv[None, :]
    cos, sin = jnp.cos(ang)[:, None, :], jnp.sin(ang)[:, None, :]
    xf = x.astype(jnp.float32)
    x1, x2 = xf[..., :half], xf[..., half:]
    return jnp.concatenate([x1 * cos - x2 * sin, x2 * cos + x1 * sin], axis=-1).astype(x.dtype)


def masked_softmax(s, mask):
    s = jnp.where(mask, s, NEG)
    m = jnp.max(s, axis=-1, keepdims=True)
    p = jnp.where(mask, jnp.exp(s - m), 0.0)
    d = jnp.sum(p, axis=-1, keepdims=True)
    return p / jnp.where(d > 0, d, 1.0)


def swiglu(x, w_gu, w_down):
    g, u = jnp.split(x @ w_gu, 2, axis=-1)
    return (jax.nn.silu(g) * u) @ w_down


def half_ffn(x, g_pre, g_post, w_gu, w_down):
    return x + 0.5 * rms_norm(swiglu(rms_norm(x, g_pre), w_gu, w_down), g_post)


def compress(k, pe, w1, b1, w2, b2):
    B, L = k.shape[:2]
    n_chunk = L // CMP_STRIDE
    ncb = n_chunk - CMP_RATIO + 1
    ch = k[:, :n_chunk * CMP_STRIDE].reshape(B, n_chunk, CMP_STRIDE, N_KV, HEAD_DIM)
    pe_r = pe.reshape(CMP_RATIO, CMP_STRIDE, HEAD_DIM)
    w1_r = w1.reshape(CMP_RATIO, CMP_STRIDE, HEAD_DIM, CMP_HIDDEN)
    hid = b1
    for r in range(CMP_RATIO):
        part = jnp.einsum('bnigd,ide->bnge', ch + pe_r[r][None, None, :, None, :], w1_r[r])
        hid = hid + part[:, r:r + ncb]
    return jax.nn.gelu(hid) @ w2 + b2


def nsa_project(h, w_in, pos):
    B, T, _ = h.shape
    proj = h @ w_in
    q = rope(proj[..., :Q_W].reshape(B, T, N_HEADS, HEAD_DIM), pos).reshape(B, T, N_KV, GROUP, HEAD_DIM)
    kv = proj[..., Q_W:Q_W + 6 * KV_W].reshape(B, T, 6, N_KV, HEAD_DIM)
    gates = jax.nn.sigmoid(proj[..., Q_W + 6 * KV_W:].astype(jnp.float32)).astype(h.dtype)
    gates = gates.reshape(B, T, N_KV, GROUP, 3)
    kc, vc = kv[:, :, 0], kv[:, :, 1]
    ks, vs = rope(kv[:, :, 2], pos), kv[:, :, 3]
    kw, vw = rope(kv[:, :, 4], pos), kv[:, :, 5]
    return q, kc, vc, ks, vs, kw, vw, gates


def nsa_context(kc, vc, ks, vs, cmp_pe, cmp_w1, cmp_b1, cmp_w2, cmp_b2):
    kcc = compress(kc, cmp_pe[0], cmp_w1[0], cmp_b1[0], cmp_w2[0], cmp_b2[0])
    vcc = compress(vc, cmp_pe[1], cmp_w1[1], cmp_b1[1], cmp_w2[1], cmp_b2[1])
    ncb = kcc.shape[1]
    kcc = rope(kcc, jnp.arange(ncb) * CMP_STRIDE + CMP_LEN - 1)
    B, L = ks.shape[:2]
    nsb = -(-L // SEL_BLOCK)
    pad = ((0, 0), (0, nsb * SEL_BLOCK - L), (0, 0), (0, 0))
    to_blocks = lambda a: jnp.pad(a, pad).reshape(B, nsb, SEL_BLOCK, N_KV, HEAD_DIM).transpose(0, 3, 1, 2, 4)
    return kcc, vcc, to_blocks(ks), to_blocks(vs)


def nsa_attend(q, q_pos, kcc, vcc, ks_blk, vs_blk, kw, vw, kw_pos, gates):
    B, T = q.shape[:2]
    scale = HEAD_DIM ** -0.5
    ncb, nsb = kcc.shape[1], ks_blk.shape[2]
    c_start = jnp.arange(ncb) * CMP_STRIDE
    c_mask = (c_start[None, :] + CMP_LEN - 1) <= q_pos[:, None]
    s_c = jnp.einsum('btgjd,bcgd->btgjc', q, kcc).astype(jnp.float32) * scale
    p_c = masked_softmax(s_c, c_mask[None, :, None, None, :])
    o_c = jnp.einsum('btgjc,bcgd->btgjd', p_c.astype(vcc.dtype), vcc)
    s_start = jnp.arange(nsb) * SEL_BLOCK
    overlap = ((c_start[:, None] < s_start[None, :] + SEL_BLOCK) &
               (c_start[:, None] + CMP_LEN > s_start[None, :])).astype(jnp.float32)
    imp = jnp.einsum('btgc,cn->btgn', jnp.sum(p_c, axis=3), overlap)
    cur = q_pos // SEL_BLOCK
    blk = jnp.arange(nsb)
    valid = blk[None, :] <= cur[:, None]
    forced = valid & ((blk[None, :] == 0) | (blk[None, :] >= cur[:, None] - 1))
    imp = jnp.where(forced[None, :, None, :], jnp.inf, jnp.where(valid[None, :, None, :], imp, -jnp.inf))
    n_top = min(SEL_TOP, nsb)
    top_v, top_i = lax.top_k(imp, n_top)
    bi = jnp.arange(B)[:, None, None, None]
    gi = jnp.arange(N_KV)[None, None, :, None]
    k_sel = ks_blk[bi, gi, top_i].reshape(B, T, N_KV, n_top * SEL_BLOCK, HEAD_DIM)
    v_sel = vs_blk[bi, gi, top_i].reshape(B, T, N_KV, n_top * SEL_BLOCK, HEAD_DIM)
    key_pos = (top_i[..., None] * SEL_BLOCK + jnp.arange(SEL_BLOCK)).reshape(B, T, N_KV, -1)
    blk_ok = jnp.broadcast_to((top_v > -jnp.inf)[..., None], top_i.shape + (SEL_BLOCK,)).reshape(B, T, N_KV, -1)
    s_mask = blk_ok & (key_pos <= q_pos[None, :, None, None])
    s_s = jnp.einsum('btgjd,btgkd->btgjk', q, k_sel).astype(jnp.float32) * scale
    p_s = masked_softmax(s_s, s_mask[:, :, :, None, :])
    o_s = jnp.einsum('btgjk,btgkd->btgjd', p_s.astype(v_sel.dtype), v_sel)
    w_mask = ((kw_pos[None, :] <= q_pos[:, None]) & (kw_pos[None, :] > q_pos[:, None] - WINDOW) &
              (kw_pos[None, :] >= 0))
    s_w = jnp.einsum('btgjd,bsgd->btgjs', q, kw).astype(jnp.float32) * scale
    p_w = masked_softmax(s_w, w_mask[None, :, None, None, :])
    o_w = jnp.einsum('btgjs,bsgd->btgjd', p_w.astype(vw.dtype), vw)
    return gates[..., 0:1] * o_c + gates[..., 1:2] * o_s + gates[..., 2:3] * o_w


def nsa_prompt(h, w_in, w_out, cmp_pe, cmp_w1, cmp_b1, cmp_w2, cmp_b2):
    B, T, _ = h.shape
    q, kc, vc, ks, vs, kw, vw, gates = nsa_project(h, w_in, jnp.arange(T))
    kcc, vcc, ks_blk, vs_blk = nsa_context(kc, vc, ks, vs, cmp_pe, cmp_w1, cmp_b1, cmp_w2, cmp_b2)
    wpad = ((0, 0), (WINDOW, 0), (0, 0), (0, 0))
    kw_pad, vw_pad = jnp.pad(kw, wpad), jnp.pad(vw, wpad)

    def q_block(i):
        s = i * Q_BLOCK
        sl = lambda a, n: lax.dynamic_slice_in_dim(a, s, n, axis=1)
        return nsa_attend(sl(q, Q_BLOCK), s + jnp.arange(Q_BLOCK), kcc, vcc, ks_blk, vs_blk,
                          sl(kw_pad, WINDOW + Q_BLOCK), sl(vw_pad, WINDOW + Q_BLOCK),
                          s - WINDOW + jnp.arange(WINDOW + Q_BLOCK), sl(gates, Q_BLOCK))

    o = lax.map(q_block, jnp.arange(T // Q_BLOCK))
    o = jnp.moveaxis(o, 0, 1).reshape(B, T, Q_W)
    n_win = min(WINDOW, T)
    return (o @ w_out, jnp.stack([kc, vc], axis=2), jnp.stack([ks, vs], axis=2),
            jnp.stack([kw[:, T - n_win:], vw[:, T - n_win:]], axis=2))


def nsa_sample(h, cmp_pool, sel_pool, win_buf, page_table, w_in, w_out, cmp_pe, cmp_w1, cmp_b1, cmp_w2, cmp_b2):
    B, T, _ = h.shape
    pos = PAST_LEN + jnp.arange(T)
    q, kc, vc, ks, vs, kw, vw, gates = nsa_project(h, w_in, pos)
    past = lambda pool: pool[page_table].reshape(B, -1, 2, N_KV, HEAD_DIM)
    cp, sp = past(cmp_pool), past(sel_pool)
    cat = lambda a, b: jnp.concatenate([a.astype(b.dtype), b], axis=1)
    kcc, vcc, ks_blk, vs_blk = nsa_context(cat(cp[:, :, 0], kc), cat(cp[:, :, 1], vc),
                                           cat(sp[:, :, 0], ks), cat(sp[:, :, 1], vs),
                                           cmp_pe, cmp_w1, cmp_b1, cmp_w2, cmp_b2)
    n_buf = win_buf.shape[1]
    kw_all, vw_all = cat(win_buf[:, :, 0], kw), cat(win_buf[:, :, 1], vw)
    kw_pos = PAST_LEN - n_buf + jnp.arange(n_buf + T)
    o = nsa_attend(q, pos, kcc, vcc, ks_blk, vs_blk, kw_all, vw_all, kw_pos, gates).reshape(B, T, Q_W)
    return (o @ w_out, jnp.stack([kc, vc], axis=2), jnp.stack([ks, vs], axis=2),
            jnp.stack([kw_all[:, T:], vw_all[:, T:]], axis=2))


def conv_module(h, state, w_pw1, b_pw1, w_dw, b_dw, ln_g, ln_b, w_pw2, b_pw2):
    a, gt = jnp.split(h @ w_pw1 + b_pw1, 2, axis=-1)
    u = a * jax.nn.sigmoid(gt)
    if state is None:
        ctx = jnp.pad(u, ((0, 0), (CONV_WIDTH - 1, 0), (0, 0)))
    else:
        ctx = jnp.concatenate([state.astype(u.dtype), u], axis=1)
    y = lax.conv_general_dilated(ctx, w_dw[:, None, :].astype(u.dtype), (1,), 'VALID',
                                 dimension_numbers=('NWC', 'WIO', 'NWC'),
                                 feature_group_count=D_INNER) + b_dw
    y = jax.nn.silu(layer_norm(y, ln_g, ln_b))
    return y @ w_pw2 + b_pw2, ctx[:, ctx.shape[1] - (CONV_WIDTH - 1):]


def setup_inputs(seed: int = 0) -> dict:
    key = jax.random.key(seed)
    ks = jax.random.split(key, 32)
    n_pages = PAST_LEN // PAGE_SIZE
    n_used = DEC_BATCH * n_pages
    n_pool = n_used + max(1, n_used // 4)
    n_buf = min(WINDOW, PAST_LEN)
    nrm = lambda k, shape, scale: scale * jax.random.normal(k, shape, jnp.float32)
    page_table = jax.random.permutation(ks[6], n_pool)[:n_used].reshape(DEC_BATCH, n_pages).astype(jnp.int32)
    return {
        'x_prompt': nrm(ks[0], (BATCH, SEQ, D_MODEL), 1.0),
        'x_sample': nrm(ks[1], (DEC_BATCH, DEC_SEQ, D_MODEL), 1.0),
        'cache_cmp_kv': nrm(ks[2], (N_ATTN, n_pool, PAGE_SIZE, 2, N_KV, HEAD_DIM), 1.0),
        'cache_sel_kv': nrm(ks[3], (N_ATTN, n_pool, PAGE_SIZE, 2, N_KV, HEAD_DIM), 1.0),
        'state_win_kv': nrm(ks[4], (N_ATTN, DEC_BATCH, n_buf, 2, N_KV, HEAD_DIM), 1.0),
        'state_conv': nrm(ks[5], (N_CONV, DEC_BATCH, CONV_WIDTH - 1, D_INNER), 0.5),
        'page_table': page_table,
        'norm_g': 1.0 + nrm(ks[7], (DEPTH, 6, D_MODEL), 0.01),
        'ffn_w_gu': nrm(ks[8], (DEPTH, 2, D_MODEL, 2 * D_FF), D_MODEL ** -0.5),
        'ffn_w_down': nrm(ks[9], (DEPTH, 2, D_FF, D_MODEL), D_FF ** -0.5),
        'attn_w_in': nrm(ks[10], (N_ATTN, D_MODEL, IN_COLS), D_MODEL ** -0.5),
        'attn_w_out': nrm(ks[11], (N_ATTN, Q_W, D_MODEL), Q_W ** -0.5),
        'cmp_pe': nrm(ks[12], (N_ATTN, 2, CMP_LEN, HEAD_DIM), 0.5),
        'cmp_w1': nrm(ks[13], (N_ATTN, 2, CMP_LEN * HEAD_DIM, CMP_HIDDEN), (CMP_LEN * HEAD_DIM) ** -0.5),
        'cmp_b1': nrm(ks[14], (N_ATTN, 2, CMP_HIDDEN), 0.01),
        'cmp_w2': nrm(ks[15], (N_ATTN, 2, CMP_HIDDEN, HEAD_DIM), CMP_HIDDEN ** -0.5),
        'cmp_b2': nrm(ks[16], (N_ATTN, 2, HEAD_DIM), 0.01),
        'conv_w_pw1': nrm(ks[17], (N_CONV, D_MODEL, 2 * D_INNER), D_MODEL ** -0.5),
        'conv_b_pw1': nrm(ks[18], (N_CONV, 2 * D_INNER), 0.01),
        'conv_w_dw': nrm(ks[19], (N_CONV, CONV_WIDTH, D_INNER), CONV_WIDTH ** -0.5),
        'conv_b_dw': nrm(ks[20], (N_CONV, D_INNER), 0.01),
        'conv_ln_g': 1.0 + nrm(ks[21], (N_CONV, D_INNER), 0.01),
        'conv_ln_b': nrm(ks[22], (N_CONV, D_INNER), 0.01),
        'conv_w_pw2': nrm(ks[23], (N_CONV, D_INNER, D_MODEL), D_INNER ** -0.5),
        'conv_b_pw2': nrm(ks[24], (N_CONV, D_MODEL), 0.01),
    }


def reference(x_prompt, x_sample, cache_cmp_kv, cache_sel_kv, state_win_kv, state_conv, page_table,
              norm_g, ffn_w_gu, ffn_w_down, attn_w_in, attn_w_out, cmp_pe, cmp_w1, cmp_b1, cmp_w2, cmp_b2,
              conv_w_pw1, conv_b_pw1, conv_w_dw, conv_b_dw, conv_ln_g, conv_ln_b, conv_w_pw2, conv_b_pw2):
    xp, xs = x_prompt, x_sample
    cmp_p, sel_p, win_p, conv_p = [], [], [], []
    cmp_s, sel_s, win_s, conv_s = [], [], [], []
    for i in range(DEPTH):
        g = norm_g[i]
        xp = half_ffn(xp, g[0], g[1], ffn_w_gu[i, 0], ffn_w_down[i, 0])
        xs = half_ffn(xs, g[0], g[1], ffn_w_gu[i, 0], ffn_w_down[i, 0])
        hp, hs = rms_norm(xp, g[2]), rms_norm(xs, g[2])
        j = i // N_MIXERS
        if i % N_MIXERS == 0:
            cmp = (cmp_pe[j], cmp_w1[j], cmp_b1[j], cmp_w2[j], cmp_b2[j])
            mp, c1, s1, w1 = nsa_prompt(hp, attn_w_in[j], attn_w_out[j], *cmp)
            ms, c2, s2, w2 = nsa_sample(hs, cache_cmp_kv[j], cache_sel_kv[j], state_win_kv[j], page_table,
                                        attn_w_in[j], attn_w_out[j], *cmp)
            cmp_p.append(c1); sel_p.append(s1); win_p.append(w1)
            cmp_s.append(c2); sel_s.append(s2); win_s.append(w2)
        else:
            cv = (conv_w_pw1[j], conv_b_pw1[j], conv_w_dw[j], conv_b_dw[j], conv_ln_g[j], conv_ln_b[j],
                  conv_w_pw2[j], conv_b_pw2[j])
            mp, st1 = conv_module(hp, None, *cv)
            ms, st2 = conv_module(hs, state_conv[j], *cv)
            conv_p.append(st1); conv_s.append(st2)
        xp = xp + rms_norm(mp, g[3])
        xs = xs + rms_norm(ms, g[3])
        xp = half_ffn(xp, g[4], g[5], ffn_w_gu[i, 1], ffn_w_down[i, 1])
        xs = half_ffn(xs, g[4], g[5], ffn_w_gu[i, 1], ffn_w_down[i, 1])
    return (xp, xs, jnp.stack(cmp_p), jnp.stack(sel_p), jnp.stack(win_p), jnp.stack(conv_p),
            jnp.stack(cmp_s), jnp.stack(sel_s), jnp.stack(win_s), jnp.stack(conv_s))
```

```python
import functools

import numpy as np
import jax
import jax.numpy as jnp
from jax import lax
from jax.experimental import pallas as pl
from jax.experimental.pallas import tpu as pltpu

F32 = jnp.float32
BF16 = jnp.bfloat16

D_MODEL = 1024
N_HEADS = 16
HEAD_DIM = 64
N_KV = 4
GROUP = N_HEADS // N_KV
KV_W = N_KV * HEAD_DIM
Q_W = N_HEADS * HEAD_DIM
N_GATE = 3 * N_HEADS
CMP_LEN = 32
CMP_STRIDE = 16
SEL_BLOCK = 64
SEL_TOP = 16
WINDOW = 512
ROPE_THETA = 10000.0
CONV_WIDTH = 31
D_FF = 2816
EPS = 1e-6
NEG = -1e30
FORCED_SCORE = 3.0e38
PAGE_SIZE = 128

Q_BLOCK = 128
KEY_TILE = 256
MAX_SEL_BLOCKS = 128
FFN_CHUNK = 256
TOKEN_TILE = 512
CONV_HALO = 32
CMP_PAGES_PER_STEP = 32
SEL_PAGES_PER_STEP = 16
VMEM_LIMIT_MB = 56


def _cparams(n_axes, vmem_mb=VMEM_LIMIT_MB):
    return pltpu.CompilerParams(dimension_semantics=("arbitrary",) * n_axes,
                                vmem_limit_bytes=vmem_mb * 1024 * 1024)


def _const_spec(shape):
    nd = len(shape)
    return pl.BlockSpec(shape, lambda *_: (0,) * nd, pipeline_mode=pl.Buffered(1))


def _dot(a, b):
    return jnp.dot(a, b, preferred_element_type=F32)


def _dot_nt(a, b):
    return lax.dot_general(a, b, (((1,), (1,)), ((), ())), preferred_element_type=F32)


def _rms(x, g):
    return x * lax.rsqrt(jnp.mean(x * x, axis=-1, keepdims=True) + EPS) * g


def _sigmoid(x):
    return 1.0 / (1.0 + jnp.exp(-x))


def _rope_lanes(x, cos, sin):
    out = []
    for c in range(x.shape[1] // 128):
        xc = x[:, c * 128:(c + 1) * 128]
        lane = lax.broadcasted_iota(jnp.int32, xc.shape, 1)
        first = (lane % HEAD_DIM) < (HEAD_DIM // 2)
        swapped = jnp.where(first, pltpu.roll(xc, 128 - HEAD_DIM // 2, 1), pltpu.roll(xc, HEAD_DIM // 2, 1))
        out.append(xc * cos + swapped * sin)
    return out[0] if len(out) == 1 else jnp.concatenate(out, axis=1)


def _masked_softmax(s, mask):
    s = jnp.where(mask, s, NEG)
    m = jnp.max(s, axis=-1, keepdims=True)
    p = jnp.where(mask, jnp.exp(s - m), 0.0)
    d = jnp.sum(p, axis=-1, keepdims=True)
    return p * (1.0 / jnp.where(d > 0, d, 1.0))


def _lane_place(x, shift):
    shift %= x.shape[1]
    return x if shift == 0 else pltpu.roll(x, shift, 1)


def _head_lane_mask(shape, slot):
    lane = lax.broadcasted_iota(jnp.int32, shape, 1)
    return (lane // HEAD_DIM) == slot


def _ffn_kernel(x_ref, gpre_ref, gpost_ref, gnext_ref, wgu_ref, wdn_ref, xo_ref, *maybe_h, emit_h):
    x = x_ref[...]
    hn = _rms(x, gpre_ref[...]).astype(BF16)
    acc = jnp.zeros(x.shape, F32)
    for c in range(D_FF // FFN_CHUNK):
        lo = c * FFN_CHUNK
        g = _dot(hn, wgu_ref[:, lo:lo + FFN_CHUNK])
        u = _dot(hn, wgu_ref[:, D_FF + lo:D_FF + lo + FFN_CHUNK])
        a = (g * _sigmoid(g) * u).astype(BF16)
        acc = acc + _dot(a, wdn_ref[lo:lo + FFN_CHUNK, :])
    y = x + 0.5 * _rms(acc, gpost_ref[...])
    xo_ref[...] = y
    if emit_h:
        maybe_h[0][...] = _rms(y, gnext_ref[...]).astype(BF16)


def _ffn(x, g_pre, g_post, g_next, wgu, wdn, emit_h):
    n = x.shape[0]
    tm = min(TOKEN_TILE, n)
    row = pl.BlockSpec((tm, D_MODEL), lambda i: (i, 0))
    out_shape = [jax.ShapeDtypeStruct((n, D_MODEL), F32)]
    out_specs = [row]
    if emit_h:
        out_shape.append(jax.ShapeDtypeStruct((n, D_MODEL), BF16))
        out_specs.append(row)
    res = pl.pallas_call(
        functools.partial(_ffn_kernel, emit_h=emit_h),
        grid=(n // tm,),
        in_specs=[row, _const_spec((1, D_MODEL)), _const_spec((1, D_MODEL)), _const_spec((1, D_MODEL)),
                  _const_spec((D_MODEL, 2 * D_FF)), _const_spec((D_FF, D_MODEL))],
        out_specs=out_specs,
        out_shape=out_shape,
        compiler_params=_cparams(1),
    )(x, g_pre[None], g_post[None], g_next[None], wgu, wdn)
    return res if emit_h else (res[0], None)


def _proj_kernel(h_ref, cos_ref, sin_ref, wq_ref, wkv_ref, wg_ref,
                 q_ref, cmp_ref, sel_ref, win_ref, gate_ref, *prompt_refs):
    h = h_ref[...]
    cos = cos_ref[...]
    sin = sin_ref[...]
    q = _rope_lanes(_dot(h, wq_ref[...]), cos, sin)
    q_ref[...] = (q * (HEAD_DIM ** -0.5)).astype(BF16)
    kv = _dot(h, wkv_ref[...])
    cmp_ref[...] = kv[:, 0:2 * KV_W]
    ks = _rope_lanes(kv[:, 2 * KV_W:3 * KV_W], cos, sin)
    vs = kv[:, 3 * KV_W:4 * KV_W]
    kw = _rope_lanes(kv[:, 4 * KV_W:5 * KV_W], cos, sin)
    vw = kv[:, 5 * KV_W:6 * KV_W]
    sel_ref[:, 0:KV_W] = ks
    sel_ref[:, KV_W:2 * KV_W] = vs
    win_ref[:, 0:KV_W] = kw
    win_ref[:, KV_W:2 * KV_W] = vw
    gate_ref[...] = _sigmoid(_dot(h, wg_ref[...]))
    if prompt_refs:
        kst_ref, vs_ref, kwt_ref, vw_ref = prompt_refs
        kst_ref[0] = ks.T.astype(BF16)
        vs_ref[0] = vs.astype(BF16)
        kwt_ref[0] = kw.T.astype(BF16)
        vw_ref[0] = vw.astype(BF16)


def _proj(h, cos, sin, wq, wkv, wg, batch=None):
    n = h.shape[0]
    tm = min(TOKEN_TILE, n)
    n_pos_tiles = cos.shape[0] // tm
    row = lambda w: pl.BlockSpec((tm, w), lambda i: (i, 0))
    tab = pl.BlockSpec((tm, 128), lambda i: (i % n_pos_tiles, 0))
    out_shape = [jax.ShapeDtypeStruct((n, Q_W), BF16), jax.ShapeDtypeStruct((n, 2 * KV_W), F32),
                 jax.ShapeDtypeStruct((n, 2 * KV_W), F32), jax.ShapeDtypeStruct((n, 2 * KV_W), F32),
                 jax.ShapeDtypeStruct((n, N_GATE), F32)]
    out_specs = [row(Q_W), row(2 * KV_W), row(2 * KV_W), row(2 * KV_W), row(N_GATE)]
    if batch is not None:
        t = n // batch
        nt = t // tm
        kt_spec = pl.BlockSpec((1, KV_W, tm), lambda i: (i // nt, 0, i % nt))
        v_spec = pl.BlockSpec((1, tm, KV_W), lambda i: (i // nt, i % nt, 0))
        out_shape += [jax.ShapeDtypeStruct((batch, KV_W, t), BF16), jax.ShapeDtypeStruct((batch, t, KV_W), BF16)] * 2
        out_specs += [kt_spec, v_spec] * 2
    return pl.pallas_call(
        _proj_kernel,
        grid=(n // tm,),
        in_specs=[row(D_MODEL), tab, tab, _const_spec((D_MODEL, Q_W)), _const_spec((D_MODEL, 6 * KV_W)),
                  _const_spec((D_MODEL, N_GATE))],
        out_specs=out_specs,
        out_shape=out_shape,
        compiler_params=_cparams(1),
    )(h, cos, sin, wq, wkv, wg)


def _compress_kernel(pt_ref, *refs, npg, nch):
    del pt_ref
    pages = refs[:npg]
    w1_ref, pe_ref, b1_ref, w2_ref, b2_ref, cos_ref, sin_ref, kt_ref, v_ref, part_ref = refs[npg:]
    j = pl.program_id(1)
    rows = npg * (PAGE_SIZE // CMP_STRIDE)
    row0 = pl.multiple_of(j * rows, rows)
    for kv in range(2):
        acc = [jnp.zeros((rows, KV_W), F32) for _ in range(CMP_LEN // CMP_STRIDE)]
        for i in range(CMP_STRIDE):
            lo = i * 2 * KV_W + kv * KV_W
            x = jnp.concatenate([p[0, :, lo:lo + KV_W] for p in pages], axis=0)
            for r in range(CMP_LEN // CMP_STRIDE):
                xr = (x + pe_ref[kv, r, i:i + 1, :]).astype(BF16)
                acc[r] = acc[r] + _dot(xr, w1_ref[kv, r, i])
        for r in range(CMP_LEN // CMP_STRIDE):
            part_ref[kv, pl.ds(row0, rows), r * KV_W:(r + 1) * KV_W] = acc[r]

    @pl.when(j == pl.num_programs(1) - 1)
    def _():
        for kv in range(2):
            hid = b1_ref[kv:kv + 1, :] + part_ref[kv, :, 0:KV_W]
            hid = hid + pltpu.roll(part_ref[kv, :, KV_W:2 * KV_W], nch - 1, 0)
            act = jax.nn.gelu(hid, approximate=True).astype(BF16)
            out = _dot(act, w2_ref[kv]) + b2_ref[kv:kv + 1, :]
            rowi = lax.broadcasted_iota(jnp.int32, out.shape, 0)
            out = jnp.where(rowi < nch - 1, out, 0.0)
            if kv == 0:
                out = _rope_lanes(out, cos_ref[...], sin_ref[...])
                kt_ref[0] = out.T.astype(BF16)
            else:
                v_ref[0] = out.astype(BF16)


def _page_index(s, j, pt, *, k, npg):
    return (pt[s, j * npg + k], 0, 0)


def _compress(pool, page_table, cw):
    s_n, p_n = page_table.shape
    chunks_per_page = PAGE_SIZE // CMP_STRIDE
    nch = p_n * chunks_per_page
    npg = min(CMP_PAGES_PER_STEP, p_n)
    row_w = CMP_STRIDE * 2 * KV_W
    pool8 = pool.reshape(pool.shape[0], chunks_per_page, row_w)
    page_specs = [pl.BlockSpec((1, chunks_per_page, row_w), functools.partial(_page_index, k=k, npg=npg))
                  for k in range(npg)]
    consts = [cw["w1"], cw["pe"], cw["b1"], cw["w2"], cw["b2"], cw["cos"], cw["sin"]]
    grid_spec = pltpu.PrefetchScalarGridSpec(
        num_scalar_prefetch=1,
        grid=(s_n, p_n // npg),
        in_specs=page_specs + [_const_spec(c.shape) for c in consts],
        out_specs=[pl.BlockSpec((1, KV_W, nch), lambda s, j, pt: (s, 0, 0)),
                   pl.BlockSpec((1, nch, KV_W), lambda s, j, pt: (s, 0, 0))],
        scratch_shapes=[pltpu.VMEM((2, nch, 2 * KV_W), F32)],
    )
    return pl.pallas_call(
        functools.partial(_compress_kernel, npg=npg, nch=nch),
        grid_spec=grid_spec,
        out_shape=[jax.ShapeDtypeStruct((s_n, KV_W, nch), BF16), jax.ShapeDtypeStruct((s_n, nch, KV_W), BF16)],
        compiler_params=_cparams(2),
    )(page_table, *([pool8] * npg), *consts)


def _rank_select(imp_t, valid, forced, n_keep, n_rank_rows, v_scr):
    v = jnp.where(forced, FORCED_SCORE, jnp.where(valid, imp_t, -1.0))
    v_scr[...] = v
    n_idx = lax.broadcasted_iota(jnp.int32, v.shape, 0)

    def body(m, cnt):
        row = v_scr[pl.ds(m, 1), :]
        ahead = (row > v) | ((row == v) & (n_idx > m))
        return cnt + jnp.where(ahead, 1, 0)

    cnt = lax.fori_loop(0, n_rank_rows, body, jnp.zeros(v.shape, jnp.int32))
    return jnp.where(valid & (cnt < n_keep), 1.0, 0.0)


def _importance_t(p_sum, ovt_ref):
    hi = p_sum.astype(BF16)
    lo = (p_sum - hi.astype(F32)).astype(BF16)
    return _dot_nt(ovt_ref[...], hi) + _dot_nt(ovt_ref[...], lo)


def _attn_prompt_kernel(q_ref, gate_ref, x_ref, kcct_ref, vcc_ref, kst_ref, vs_ref, kwt_ref, vw_ref,
                        e_ref, ovt_ref, wout_ref, g3_ref, o_ref, v_scr):
    qb = pl.program_id(1)
    q0 = qb * Q_BLOCK
    rows = GROUP * Q_BLOCK

    def row_pos(shape):
        return q0 + (lax.broadcasted_iota(jnp.int32, shape, 0) % Q_BLOCK)

    slabs = []
    for g in range(N_KV):
        qg = q_ref[0, :, g * KV_W:(g + 1) * KV_W].astype(F32)
        g_lanes = _head_lane_mask(qg.shape, g)
        qm = jnp.concatenate(
            [jnp.where(g_lanes, _lane_place(qg, (g - j) * HEAD_DIM), 0.0) for j in range(GROUP)], axis=0
        ).astype(BF16)

        s_c = _dot(qm, kcct_ref[0])
        c_end = lax.broadcasted_iota(jnp.int32, s_c.shape, 1) * CMP_STRIDE + (CMP_LEN - 1)
        p_c = _masked_softmax(s_c, c_end <= row_pos(s_c.shape))
        o_c = _dot(p_c.astype(BF16), vcc_ref[0])

        p_sum = p_c[0:Q_BLOCK]
        for j in range(1, GROUP):
            p_sum = p_sum + p_c[j * Q_BLOCK:(j + 1) * Q_BLOCK]
        imp_t = _importance_t(p_sum, ovt_ref)
        n_idx = lax.broadcasted_iota(jnp.int32, imp_t.shape, 0)
        cur = (q0 + lax.broadcasted_iota(jnp.int32, imp_t.shape, 1)) // SEL_BLOCK
        valid = n_idx <= cur
        forced = valid & ((n_idx == 0) | (n_idx >= cur - 1))
        n_live = jnp.minimum((q0 + Q_BLOCK) // SEL_BLOCK, MAX_SEL_BLOCKS)
        sel = _rank_select(imp_t, valid, forced, SEL_TOP, n_live, v_scr).T.astype(BF16)

        def sel_step(kt, carry):
            m_old, l_old, acc = carry
            k0 = pl.multiple_of(kt * KEY_TILE, KEY_TILE)
            s = _dot(qm, kst_ref[0, :, pl.ds(k0, KEY_TILE)])
            picked = _dot(sel, e_ref[:, pl.ds(k0, KEY_TILE)])
            key_pos = k0 + lax.broadcasted_iota(jnp.int32, picked.shape, 1)
            tok_pos = q0 + lax.broadcasted_iota(jnp.int32, picked.shape, 0)
            bias = jnp.where((picked > 0.5) & (key_pos <= tok_pos), 0.0, NEG)
            s = (s.reshape(GROUP, Q_BLOCK, KEY_TILE) + bias[None]).reshape(rows, KEY_TILE)
            m_new = jnp.maximum(m_old, jnp.max(s, axis=-1, keepdims=True))
            alpha = jnp.exp(m_old - m_new)
            p = jnp.exp(s - m_new)
            l_new = alpha * l_old + jnp.sum(p, axis=-1, keepdims=True)
            acc = alpha * acc + _dot(p.astype(BF16), vs_ref[0, pl.ds(k0, KEY_TILE), :])
            return m_new, l_new, acc

        n_tiles = (q0 + Q_BLOCK + KEY_TILE - 1) // KEY_TILE
        init = (jnp.full((rows, 1), NEG, F32), jnp.zeros((rows, 1), F32), jnp.zeros((rows, KV_W), F32))
        _, l_s, acc_s = lax.fori_loop(0, n_tiles, sel_step, init)
        o_s = acc_s * (1.0 / l_s)

        wlen = WINDOW + Q_BLOCK
        w0 = pl.multiple_of(jnp.maximum(q0 - WINDOW, 0), Q_BLOCK)
        s_w = _dot(qm, kwt_ref[0, :, pl.ds(w0, wlen)])
        key_pos = w0 + lax.broadcasted_iota(jnp.int32, s_w.shape, 1)
        tok_pos = row_pos(s_w.shape)
        p_w = _masked_softmax(s_w, (key_pos <= tok_pos) & (key_pos > tok_pos - WINDOW))
        o_w = _dot(p_w.astype(BF16), vw_ref[0, pl.ds(w0, wlen), :])

        slab = jnp.zeros((Q_BLOCK, KV_W), F32)
        for j in range(GROUP):
            c = (g * GROUP + j) * 3
            rs = slice(j * Q_BLOCK, (j + 1) * Q_BLOCK)
            mix = (gate_ref[0, :, c:c + 1] * o_c[rs] + gate_ref[0, :, c + 1:c + 2] * o_s[rs]
                   + gate_ref[0, :, c + 2:c + 3] * o_w[rs])
            slab = slab + jnp.where(_head_lane_mask(mix.shape, j), _lane_place(mix, (j - g) * HEAD_DIM), 0.0)
        slabs.append(slab)
    o = jnp.concatenate(slabs, axis=1).astype(BF16)
    o_ref[0] = x_ref[0] + _rms(_dot(o, wout_ref[...]), g3_ref[...])


def _attn_prompt(q, gates, x, kcct, vcc, kst, vs, kwt, vw, e_mat, ovt, wout, g3):
    b, t, _ = q.shape
    nch = kcct.shape[2]
    per_q = lambda w: pl.BlockSpec((1, Q_BLOCK, w), lambda bi, qi: (bi, qi, 0))
    per_b = lambda s: pl.BlockSpec((1,) + s, lambda bi, qi: (bi, 0, 0), pipeline_mode=pl.Buffered(1))
    return pl.pallas_call(
        _attn_prompt_kernel,
        grid=(b, t // Q_BLOCK),
        in_specs=[per_q(Q_W), per_q(N_GATE), per_q(D_MODEL),
                  per_b((KV_W, nch)), per_b((nch, KV_W)),
                  per_b((KV_W, t)), per_b((t, KV_W)), per_b((KV_W, t)), per_b((t, KV_W)),
                  _const_spec(e_mat.shape), _const_spec(ovt.shape), _const_spec((Q_W, D_MODEL)),
                  _const_spec((1, D_MODEL))],
        out_specs=per_q(D_MODEL),
        out_shape=jax.ShapeDtypeStruct((b, t, D_MODEL), F32),
        scratch_shapes=[pltpu.VMEM((MAX_SEL_BLOCKS, Q_BLOCK), F32)],
        compiler_params=_cparams(2),
    )(q, gates, x, kcct, vcc, kst, vs, kwt, vw, e_mat, ovt, wout, g3[None])


TOK_PAD = 8


def _attn_sample_kernel(pt_ref, q_ref, gate_ref, x_ref, kcct_ref, vcc_ref, selnew_ref, winnew_ref, winbuf_ref,
                        *refs, npg, n_tok, past):
    del pt_ref
    pages = refs[:npg]
    (e_ref, ovt_ref, wout_ref, g3_ref, o_ref,
     qexp_scr, selrow_scr, m_scr, l_scr, acc_scr, oc_scr, v_scr) = refs[npg:]
    k = pl.program_id(1)
    rows = N_HEADS * TOK_PAD
    n_cache_blocks = past // SEL_BLOCK

    def row_tok(shape):
        return lax.broadcasted_iota(jnp.int32, shape, 0) % TOK_PAD

    def pad_rows(a, n):
        return jnp.concatenate([a, jnp.zeros((n - a.shape[0], a.shape[1]), a.dtype)], axis=0)

    @pl.when(k == 0)
    def _():
        q8 = q_ref[0].astype(F32)
        pieces = []
        for g in range(N_KV):
            qg = q8[:, g * KV_W:(g + 1) * KV_W]
            g_lanes = _head_lane_mask(qg.shape, g)
            for j in range(GROUP):
                pieces.append(jnp.where(g_lanes, _lane_place(qg, (g - j) * HEAD_DIM), 0.0))
        qexp = jnp.concatenate(pieces, axis=0).astype(BF16)
        qexp_scr[...] = qexp

        s_c = _dot(qexp, kcct_ref[0])
        c_end = lax.broadcasted_iota(jnp.int32, s_c.shape, 1) * CMP_STRIDE + (CMP_LEN - 1)
        p_c = _masked_softmax(s_c, c_end <= past + row_tok(s_c.shape))
        oc_scr[...] = _dot(p_c.astype(BF16), vcc_ref[0])

        sums = []
        for g in range(N_KV):
            acc = p_c[(g * GROUP) * TOK_PAD:(g * GROUP + 1) * TOK_PAD]
            for j in range(1, GROUP):
                acc = acc + p_c[(g * GROUP + j) * TOK_PAD:(g * GROUP + j + 1) * TOK_PAD]
            sums.append(acc)
        p_sum = pad_rows(jnp.concatenate(sums, axis=0), MAX_SEL_BLOCKS)
        imp_t = _importance_t(p_sum, ovt_ref)
        n_idx = lax.broadcasted_iota(jnp.int32, imp_t.shape, 0)
        cur = (past + lax.broadcasted_iota(jnp.int32, imp_t.shape, 1) % TOK_PAD) // SEL_BLOCK
        valid = n_idx <= cur
        forced = valid & ((n_idx == 0) | (n_idx >= cur - 1))
        sel = _rank_select(imp_t, valid, forced, SEL_TOP - 1, n_cache_blocks, v_scr).T
        sel_rows = []
        for g in range(N_KV):
            sel_rows += [sel[g * TOK_PAD:(g + 1) * TOK_PAD]] * GROUP
        selrow_scr[...] = jnp.concatenate(sel_rows, axis=0).astype(BF16)
        m_scr[...] = jnp.full(m_scr.shape, NEG, F32)
        l_scr[...] = jnp.zeros(l_scr.shape, F32)
        acc_scr[...] = jnp.zeros(acc_scr.shape, F32)

    def online_update(s, v_tiles):
        m_old = m_scr[...]
        m_new = jnp.maximum(m_old, jnp.max(s, axis=-1, keepdims=True))
        alpha = jnp.exp(m_old - m_new)
        p = jnp.exp(s - m_new)
        l_scr[...] = alpha * l_scr[...] + jnp.sum(p, axis=-1, keepdims=True)
        acc = alpha * acc_scr[...]
        w = s.shape[1] // len(v_tiles)
        for i, v in enumerate(v_tiles):
            acc = acc + _dot(p[:, i * w:(i + 1) * w].astype(BF16), v)
        acc_scr[...] = acc
        m_scr[...] = m_new

    qexp = qexp_scr[...]
    keys = npg * PAGE_SIZE
    k0 = pl.multiple_of(k * keys, keys)
    s = jnp.concatenate([_dot_nt(qexp, p[0, :, 0:KV_W].astype(BF16)) for p in pages], axis=1)
    picked = _dot(selrow_scr[...], e_ref[:, pl.ds(k0, keys)])
    s = s + jnp.where(picked > 0.5, 0.0, NEG)
    online_update(s, [p[0, :, KV_W:2 * KV_W].astype(BF16) for p in pages])

    @pl.when(k == pl.num_programs(1) - 1)
    def _():
        tok = row_tok((rows, PAGE_SIZE))
        new_i = lax.broadcasted_iota(jnp.int32, (rows, PAGE_SIZE), 1)
        new_ok = (new_i < n_tok) & (new_i <= tok)

        sel_new = pad_rows(selnew_ref[0], PAGE_SIZE)
        s_n = _dot_nt(qexp, sel_new[:, 0:KV_W].astype(BF16))
        online_update(jnp.where(new_ok, s_n, NEG), [sel_new[:, KV_W:2 * KV_W].astype(BF16)])
        o_s = acc_scr[...] * (1.0 / l_scr[...])

        nbuf = winbuf_ref.shape[1]
        win_new = pad_rows(winnew_ref[0], PAGE_SIZE)
        s_b = _dot_nt(qexp, winbuf_ref[0, :, 0:KV_W].astype(BF16))
        s_nw = _dot_nt(qexp, win_new[:, 0:KV_W].astype(BF16))
        s_w = jnp.concatenate([s_b, s_nw], axis=1)
        col = lax.broadcasted_iota(jnp.int32, s_w.shape, 1)
        tok_w = row_tok(s_w.shape)
        in_buf = (col < nbuf) & (past - nbuf + col > past + tok_w - WINDOW) & (past - nbuf + col >= 0)
        in_new = (col >= nbuf) & (col - nbuf < n_tok) & (col - nbuf <= tok_w)
        p_w = _masked_softmax(s_w, in_buf | in_new)
        o_w = (_dot(p_w[:, 0:nbuf].astype(BF16), winbuf_ref[0, :, KV_W:2 * KV_W].astype(BF16))
               + _dot(p_w[:, nbuf:].astype(BF16), win_new[:, KV_W:2 * KV_W].astype(BF16)))

        o_c = oc_scr[...]
        gates = gate_ref[0]
        slabs = []
        for g in range(N_KV):
            slab = jnp.zeros((TOK_PAD, KV_W), F32)
            for j in range(GROUP):
                c = (g * GROUP + j) * 3
                rs = slice((g * GROUP + j) * TOK_PAD, (g * GROUP + j + 1) * TOK_PAD)
                mix = gates[:, c:c + 1] * o_c[rs] + gates[:, c + 1:c + 2] * o_s[rs] + gates[:, c + 2:c + 3] * o_w[rs]
                slab = slab + jnp.where(_head_lane_mask(mix.shape, j), _lane_place(mix, (j - g) * HEAD_DIM), 0.0)
            slabs.append(slab)
        o = jnp.concatenate(slabs, axis=1).astype(BF16)
        y = _rms(_dot(o, wout_ref[...]), g3_ref[...])
        o_ref[0] = x_ref[0] + y


def _attn_sample(q, gates, x, kcct, vcc, sel_new, win_new, win_buf, sel_pool, page_table, e_mat, ovt, wout, g3,
                 past):
    s_n, n_tok, _ = q.shape
    pad_tok = lambda a: jnp.pad(a, ((0, 0), (0, TOK_PAD - n_tok), (0, 0)))
    q, gates, x, sel_new, win_new = (pad_tok(a) for a in (q, gates, x, sel_new, win_new))
    p_n = page_table.shape[1]
    npg = min(SEL_PAGES_PER_STEP, p_n)
    nch = kcct.shape[2]
    nbuf = win_buf.shape[1]
    rows = N_HEADS * TOK_PAD
    per_s = lambda shp: pl.BlockSpec((1,) + shp, lambda s, k, pt: (s, 0, 0))
    page_specs = [pl.BlockSpec((1, PAGE_SIZE, 2 * KV_W), functools.partial(_page_index, k=i, npg=npg))
                  for i in range(npg)]
    grid_spec = pltpu.PrefetchScalarGridSpec(
        num_scalar_prefetch=1,
        grid=(s_n, p_n // npg),
        in_specs=[per_s((TOK_PAD, Q_W)), per_s((TOK_PAD, N_GATE)), per_s((TOK_PAD, D_MODEL)),
                  per_s((KV_W, nch)), per_s((nch, KV_W)), per_s((TOK_PAD, 2 * KV_W)), per_s((TOK_PAD, 2 * KV_W)),
                  per_s((nbuf, 2 * KV_W))] + page_specs +
                 [_const_spec(e_mat.shape), _const_spec(ovt.shape), _const_spec((Q_W, D_MODEL)),
                  _const_spec((1, D_MODEL))],
        out_specs=per_s((TOK_PAD, D_MODEL)),
        scratch_shapes=[pltpu.VMEM((rows, KV_W), BF16), pltpu.VMEM((rows, MAX_SEL_BLOCKS), BF16),
                        pltpu.VMEM((rows, 1), F32), pltpu.VMEM((rows, 1), F32), pltpu.VMEM((rows, KV_W), F32),
                        pltpu.VMEM((rows, KV_W), F32), pltpu.VMEM((MAX_SEL_BLOCKS, MAX_SEL_BLOCKS), F32)],
    )
    return pl.pallas_call(
        functools.partial(_attn_sample_kernel, npg=npg, n_tok=n_tok, past=past),
        grid_spec=grid_spec,
        out_shape=jax.ShapeDtypeStruct((s_n, TOK_PAD, D_MODEL), F32),
        compiler_params=_cparams(2),
    )(page_table, q, gates, x, kcct, vcc, sel_new, win_new, win_buf, *([sel_pool] * npg), e_mat, ovt, wout,
      g3[None])[:, :n_tok]


def _conv_tail(y, lng_ref, lnb_ref, w2_ref, b2_ref, g3_ref):
    mean = jnp.mean(y, axis=-1, keepdims=True)
    yc = y - mean
    yn = yc * lax.rsqrt(jnp.mean(yc * yc, axis=-1, keepdims=True) + EPS) * lng_ref[...] + lnb_ref[...]
    act = (yn * _sigmoid(yn)).astype(BF16)
    return _rms(_dot(act, w2_ref[...]) + b2_ref[...], g3_ref[...])


def _glu(h, w1_ref, b1_ref):
    ag = _dot(h, w1_ref[...]) + b1_ref[...]
    d = ag.shape[1] // 2
    return ag[:, 0:d] * _sigmoid(ag[:, d:])


def _conv_prompt_kernel(h_ref, x_ref, w1_ref, b1_ref, wdw_ref, bdw_ref, lng_ref, lnb_ref, w2_ref, b2_ref, g3_ref,
                        o_ref, st_ref, ctx_scr):
    ti = pl.program_id(1)
    tm = h_ref.shape[1]

    @pl.when(ti == 0)
    def _():
        ctx_scr[0:CONV_HALO, :] = jnp.zeros((CONV_HALO, ctx_scr.shape[1]), F32)

    ctx_scr[CONV_HALO:CONV_HALO + tm, :] = _glu(h_ref[0], w1_ref, b1_ref)
    off = CONV_HALO - (CONV_WIDTH - 1)
    y = jnp.zeros((tm, ctx_scr.shape[1]), F32)
    for kk in range(CONV_WIDTH):
        y = y + ctx_scr[off + kk:off + kk + tm, :] * wdw_ref[kk:kk + 1, :]
    y = y + bdw_ref[...]
    o_ref[0] = x_ref[0] + _conv_tail(y, lng_ref, lnb_ref, w2_ref, b2_ref, g3_ref)

    @pl.when(ti == pl.num_programs(1) - 1)
    def _():
        st_ref[0] = ctx_scr[tm + off:tm + CONV_HALO, :]

    ctx_scr[0:CONV_HALO, :] = ctx_scr[tm:tm + CONV_HALO, :]


def _conv_prompt(h, x, cv, g3):
    b, t, _ = h.shape
    d_in = cv["wdw"].shape[1]
    tm = min(TOKEN_TILE, t)
    row = pl.BlockSpec((1, tm, D_MODEL), lambda bi, ti: (bi, ti, 0))
    consts = [cv["w1"], cv["b1"], cv["wdw"], cv["bdw"], cv["lng"], cv["lnb"], cv["w2"], cv["b2"], g3[None]]
    return pl.pallas_call(
        _conv_prompt_kernel,
        grid=(b, t // tm),
        in_specs=[row, row] + [_const_spec(c.shape) for c in consts],
        out_specs=[row, pl.BlockSpec((1, CONV_WIDTH - 1, d_in), lambda bi, ti: (bi, 0, 0))],
        out_shape=[jax.ShapeDtypeStruct((b, t, D_MODEL), F32),
                   jax.ShapeDtypeStruct((b, CONV_WIDTH - 1, d_in), F32)],
        scratch_shapes=[pltpu.VMEM((tm + CONV_HALO, d_in), F32)],
        compiler_params=_cparams(2),
    )(h, x, *consts)


def _conv_sample_kernel(h_ref, x_ref, st_ref, w1_ref, b1_ref, wdw_ref, bdw_ref, lng_ref, lnb_ref, w2_ref, b2_ref,
                        g3_ref, o_ref, sto_ref, *, n_tok, n_seq):
    u = _glu(h_ref[...], w1_ref, b1_ref)
    n_state = CONV_WIDTH - 1

    def ctx(i):
        return st_ref[i] if i < n_state else u[(i - n_state) * n_seq:(i - n_state + 1) * n_seq]

    ys = []
    for t in range(n_tok):
        y = jnp.zeros((n_seq, u.shape[1]), F32)
        for kk in range(CONV_WIDTH):
            y = y + ctx(t + kk) * wdw_ref[kk:kk + 1, :]
        ys.append(y)
    y = jnp.concatenate(ys, axis=0) + bdw_ref[...]
    o_ref[...] = x_ref[...] + _conv_tail(y, lng_ref, lnb_ref, w2_ref, b2_ref, g3_ref)
    for i in range(n_state):
        sto_ref[i] = ctx(i + n_tok)


def _conv_sample(h_ts, x_ts, state_t, cv, g3, n_tok):
    n, _ = h_ts.shape
    n_seq = n // n_tok
    d_in = cv["wdw"].shape[1]
    args = [h_ts, x_ts, state_t, cv["w1"], cv["b1"], cv["wdw"], cv["bdw"], cv["lng"], cv["lnb"], cv["w2"],
            cv["b2"], g3[None]]
    return pl.pallas_call(
        functools.partial(_conv_sample_kernel, n_tok=n_tok, n_seq=n_seq),
        grid=(1,),
        in_specs=[_const_spec(a.shape) for a in args],
        out_specs=[pl.BlockSpec((n, D_MODEL), lambda i: (0, 0)),
                   pl.BlockSpec((CONV_WIDTH - 1, n_seq, d_in), lambda i: (0, 0, 0))],
        out_shape=[jax.ShapeDtypeStruct((n, D_MODEL), F32),
                   jax.ShapeDtypeStruct((CONV_WIDTH - 1, n_seq, d_in), F32)],
        compiler_params=_cparams(1),
    )(*args)


def _rope_tables(pos):
    half = HEAD_DIM // 2
    inv = ROPE_THETA ** (-jnp.arange(half, dtype=F32) / half)
    ang = pos.astype(F32)[:, None] * inv[None, :]
    cos, sin = jnp.cos(ang), jnp.sin(ang)
    return jnp.concatenate([cos, cos, cos, cos], axis=1), jnp.concatenate([-sin, sin, -sin, sin], axis=1)


def _block_expand_matrix(n_keys):
    n = np.arange(MAX_SEL_BLOCKS)[:, None]
    k = np.arange(n_keys)[None, :]
    return jnp.asarray((k // SEL_BLOCK == n).astype(np.float32), dtype=BF16)


def _overlap_t(nch):
    n = np.arange(MAX_SEL_BLOCKS)[:, None] * SEL_BLOCK
    c = np.arange(nch)[None, :] * CMP_STRIDE
    ov = (c < n + SEL_BLOCK) & (c + CMP_LEN > n) & (np.arange(nch)[None, :] < nch - 1)
    return jnp.asarray(ov.astype(np.float32), dtype=BF16)


def _compress_weights(pe, w1, b1, w2, b2, nch):
    ratio = CMP_LEN // CMP_STRIDE
    eye = jnp.eye(N_KV, dtype=F32)
    w1r = w1.reshape(2, ratio, CMP_STRIDE, HEAD_DIM, HEAD_DIM)
    w1bd = jnp.einsum("gh,kride->krigdhe", eye, w1r).reshape(2, ratio, CMP_STRIDE, KV_W, KV_W).astype(BF16)
    w2bd = jnp.einsum("gh,kde->kgdhe", eye, w2).reshape(2, KV_W, KV_W).astype(BF16)
    cos, sin = _rope_tables(jnp.arange(nch) * CMP_STRIDE + CMP_LEN - 1)
    return {
        "w1": w1bd,
        "pe": jnp.tile(pe.reshape(2, ratio, CMP_STRIDE, HEAD_DIM), (1, 1, 1, N_KV)),
        "b1": jnp.tile(b1, (1, N_KV)),
        "w2": w2bd,
        "b2": jnp.tile(b2, (1, N_KV)),
        "cos": cos,
        "sin": sin,
    }


def _nsa_layer(hp, xp, hs, xs, cmp_pool, sel_pool, win_buf, page_table, w_in, w_out, pe, w1, b1, w2, b2, g3):
    b, t, _ = xp.shape
    s_n, n_tok, _ = xs.shape
    past = page_table.shape[1] * PAGE_SIZE
    assert t % TOKEN_TILE == 0 and t // SEL_BLOCK <= MAX_SEL_BLOCKS and t >= WINDOW + Q_BLOCK
    assert past // SEL_BLOCK == MAX_SEL_BLOCKS and n_tok <= TOK_PAD and past % SEL_BLOCK == 0
    wq = w_in[:, :Q_W].astype(BF16)
    wkv = w_in[:, Q_W:Q_W + 6 * KV_W].astype(BF16)
    wg = w_in[:, Q_W + 6 * KV_W:].astype(BF16)
    wout = w_out.astype(BF16)
    nch_p = t // CMP_STRIDE
    cw = _compress_weights(pe, w1, b1, w2, b2, nch_p)

    cos_p, sin_p = _rope_tables(jnp.arange(t))
    q, cmp_kv, sel_kv, win_kv, gates, kst, vsb, kwt, vwb = _proj(hp.reshape(b * t, D_MODEL), cos_p, sin_p, wq, wkv,
                                                                  wg, batch=b)
    ident_pages = jnp.arange(b * t // PAGE_SIZE, dtype=jnp.int32).reshape(b, t // PAGE_SIZE)
    kcct, vcc = _compress(cmp_kv.reshape(b * t // PAGE_SIZE, PAGE_SIZE, 2 * KV_W), ident_pages, cw)
    xp_new = _attn_prompt(q.reshape(b, t, Q_W), gates.reshape(b, t, N_GATE), xp, kcct, vcc, kst, vsb, kwt, vwb,
                          _block_expand_matrix(t), _overlap_t(nch_p), wout, g3)
    kv6 = lambda a, lead: a.reshape(lead + (2, N_KV, HEAD_DIM))
    n_win = min(WINDOW, t)
    outs_p = (kv6(cmp_kv, (b, t)), kv6(sel_kv, (b, t)), kv6(win_kv, (b, t))[:, t - n_win:])

    nch_s = past // CMP_STRIDE
    cw_s = cw if nch_s == nch_p else _compress_weights(pe, w1, b1, w2, b2, nch_s)
    cos_s, sin_s = _rope_tables(past + jnp.arange(s_n * n_tok) % n_tok)
    q_s, cmp_s, sel_s, win_s, gates_s = _proj(hs.reshape(s_n * n_tok, D_MODEL), cos_s, sin_s, wq, wkv, wg)
    n_pool = cmp_pool.shape[0]
    kcct_s, vcc_s = _compress(cmp_pool.reshape(n_pool, PAGE_SIZE, 2 * KV_W), page_table, cw_s)
    per_seq = lambda a: a.reshape(s_n, n_tok, a.shape[-1])
    nbuf = win_buf.shape[1]
    win_buf2 = win_buf.reshape(s_n, nbuf, 2 * KV_W)
    xs_new = _attn_sample(per_seq(q_s), per_seq(gates_s), xs, kcct_s, vcc_s, per_seq(sel_s), per_seq(win_s),
                          win_buf2, sel_pool.reshape(n_pool, PAGE_SIZE, 2 * KV_W), page_table,
                          _block_expand_matrix(past), _overlap_t(nch_s), wout, g3, past)
    win_all = jnp.concatenate([win_buf2, per_seq(win_s)], axis=1)[:, n_tok:]
    outs_s = (kv6(cmp_s, (s_n, n_tok)), kv6(sel_s, (s_n, n_tok)), kv6(win_all, (s_n, nbuf)))
    return xp_new, xs_new, outs_p, outs_s


def _conv_layer(hp, xp, hs, xs, state, w_pw1, b_pw1, w_dw, b_dw, ln_g, ln_b, w_pw2, b_pw2, g3):
    s_n, n_tok, _ = xs.shape
    cv = {"w1": w_pw1.astype(BF16), "b1": b_pw1[None], "wdw": w_dw, "bdw": b_dw[None], "lng": ln_g[None],
          "lnb": ln_b[None], "w2": w_pw2.astype(BF16), "b2": b_pw2[None]}
    xp_new, st_p = _conv_prompt(hp, xp, cv, g3)
    to_ts = lambda a: a.reshape(s_n, n_tok, -1).transpose(1, 0, 2).reshape(s_n * n_tok, -1)
    xs_ts, st_t = _conv_sample(to_ts(hs), to_ts(xs), state.transpose(1, 0, 2), cv, g3, n_tok)
    xs_new = xs_ts.reshape(n_tok, s_n, -1).transpose(1, 0, 2)
    return xp_new, xs_new, st_p, st_t.transpose(1, 0, 2)


def kernel(x_prompt, x_sample, cache_cmp_kv, cache_sel_kv, state_win_kv, state_conv, page_table, norm_g, ffn_w_gu,
           ffn_w_down, attn_w_in, attn_w_out, cmp_pe, cmp_w1, cmp_b1, cmp_w2, cmp_b2, conv_w_pw1, conv_b_pw1,
           conv_w_dw, conv_b_dw, conv_ln_g, conv_ln_b, conv_w_pw2, conv_b_pw2):
    b, t, d = x_prompt.shape
    s_n, n_tok, _ = x_sample.shape
    depth = norm_g.shape[0]
    xp = x_prompt.reshape(b * t, d)
    xs = x_sample.reshape(s_n * n_tok, d)
    cmp_p, sel_p, win_p, conv_p = [], [], [], []
    cmp_s, sel_s, win_s, conv_s = [], [], [], []
    for i in range(depth):
        g = norm_g[i]
        wgu, wdn = ffn_w_gu[i, 0].astype(BF16), ffn_w_down[i, 0].astype(BF16)
        xp, hp = _ffn(xp, g[0], g[1], g[2], wgu, wdn, True)
        xs, hs = _ffn(xs, g[0], g[1], g[2], wgu, wdn, True)
        j = i // 2
        xp3, xs3 = xp.reshape(b, t, d), xs.reshape(s_n, n_tok, d)
        hp3, hs3 = hp.reshape(b, t, d), hs.reshape(s_n, n_tok, d)
        if i % 2 == 0:
            xp3, xs3, o_p, o_s = _nsa_layer(hp3, xp3, hs3, xs3, cache_cmp_kv[j], cache_sel_kv[j], state_win_kv[j],
                                            page_table, attn_w_in[j], attn_w_out[j], cmp_pe[j], cmp_w1[j], cmp_b1[j],
                                            cmp_w2[j], cmp_b2[j], g[3])
            cmp_p.append(o_p[0]); sel_p.append(o_p[1]); win_p.append(o_p[2])
            cmp_s.append(o_s[0]); sel_s.append(o_s[1]); win_s.append(o_s[2])
        else:
            xp3, xs3, st_p, st_s = _conv_layer(hp3, xp3, hs3, xs3, state_conv[j], conv_w_pw1[j], conv_b_pw1[j],
                                               conv_w_dw[j], conv_b_dw[j], conv_ln_g[j], conv_ln_b[j], conv_w_pw2[j],
                                               conv_b_pw2[j], g[3])
            conv_p.append(st_p); conv_s.append(st_s)
        xp, xs = xp3.reshape(b * t, d), xs3.reshape(s_n * n_tok, d)
        wgu, wdn = ffn_w_gu[i, 1].astype(BF16), ffn_w_down[i, 1].astype(BF16)
        xp, _ = _ffn(xp, g[4], g[5], g[5], wgu, wdn, False)
        xs, _ = _ffn(xs, g[4], g[5], g[5], wgu, wdn, False)
    return (xp.reshape(b, t, d), xs.reshape(s_n, n_tok, d), jnp.stack(cmp_p), jnp.stack(sel_p), jnp.stack(win_p),
            jnp.stack(conv_p), jnp.stack(cmp_s), jnp.stack(sel_s), jnp.stack(win_s), jnp.stack(conv_s))
```

```python
import functools

import numpy as np
import jax
import jax.numpy as jnp
from jax import lax
from jax.experimental import pallas as pl
from jax.experimental.pallas import tpu as pltpu

F32 = jnp.float32
BF16 = jnp.bfloat16

D_MODEL = 1024
N_HEADS = 16
HEAD_DIM = 64
N_KV = 4
GROUP = N_HEADS // N_KV
KV_W = N_KV * HEAD_DIM
Q_W = N_HEADS * HEAD_DIM
N_GATE = 3 * N_HEADS
CMP_LEN = 32
CMP_STRIDE = 16
SEL_BLOCK = 64
SEL_TOP = 16
WINDOW = 512
ROPE_THETA = 10000.0
CONV_WIDTH = 31
D_FF = 2816
EPS = 1e-6
NEG = -1e30
FORCED_KEY = 0x7F000000
LOG2E = 1.4426950408889634
Q_SCALE = HEAD_DIM ** -0.5 * LOG2E
PAGE_SIZE = 128

Q_BLOCK = 128
KEY_TILE = 256
MAX_SEL_BLOCKS = 128
FFN_CHUNK = 256
TOKEN_TILE = 512
CONV_HALO = 32
CMP_PAGES_PER_STEP = 32
SEL_PAGES_PER_STEP = 16
VMEM_LIMIT_MB = 56


def _cparams(n_axes, vmem_mb=VMEM_LIMIT_MB):
    return pltpu.CompilerParams(dimension_semantics=("arbitrary",) * n_axes,
                                vmem_limit_bytes=vmem_mb * 1024 * 1024)


def _const_spec(shape):
    nd = len(shape)
    return pl.BlockSpec(shape, lambda *_: (0,) * nd, pipeline_mode=pl.Buffered(1))


def _dot(a, b):
    return jnp.dot(a, b, preferred_element_type=F32)


def _dot_nt(a, b):
    return lax.dot_general(a, b, (((1,), (1,)), ((), ())), preferred_element_type=F32)


def _rms(x, g):
    return x * lax.rsqrt(jnp.mean(x * x, axis=-1, keepdims=True) + EPS) * g


def _sigmoid(x):
    return 1.0 / (1.0 + jnp.exp(-x))


def _rope_lanes(x, cos, sin):
    out = []
    for c in range(x.shape[1] // 128):
        xc = x[:, c * 128:(c + 1) * 128]
        lane = lax.broadcasted_iota(jnp.int32, xc.shape, 1)
        first = (lane % HEAD_DIM) < (HEAD_DIM // 2)
        swapped = jnp.where(first, pltpu.roll(xc, 128 - HEAD_DIM // 2, 1), pltpu.roll(xc, HEAD_DIM // 2, 1))
        out.append(xc * cos + swapped * sin)
    return out[0] if len(out) == 1 else jnp.concatenate(out, axis=1)


def _masked_softmax2(s, mask):
    s = jnp.where(mask, s, NEG)
    m = jnp.max(s, axis=-1, keepdims=True)
    p = jnp.where(mask, jnp.exp2(s - m), 0.0)
    d = jnp.sum(p, axis=-1, keepdims=True)
    return p * (1.0 / jnp.where(d > 0, d, 1.0))


def _lane_place(x, shift):
    shift %= x.shape[1]
    return x if shift == 0 else pltpu.roll(x, shift, 1)


def _head_lane_mask(shape, slot):
    lane = lax.broadcasted_iota(jnp.int32, shape, 1)
    return (lane // HEAD_DIM) == slot


def _ffn_kernel(x_ref, gpre_ref, gpost_ref, gnext_ref, wgu_ref, wdn_ref, xo_ref, *maybe_h, emit_h):
    x = x_ref[...]
    hn = _rms(x, gpre_ref[...]).astype(BF16)
    acc = jnp.zeros(x.shape, F32)
    for c in range(D_FF // FFN_CHUNK):
        lo = c * FFN_CHUNK
        g = _dot(hn, wgu_ref[:, lo:lo + FFN_CHUNK])
        u = _dot(hn, wgu_ref[:, D_FF + lo:D_FF + lo + FFN_CHUNK])
        a = (g * _sigmoid(g) * u).astype(BF16)
        acc = acc + _dot(a, wdn_ref[lo:lo + FFN_CHUNK, :])
    y = x + 0.5 * _rms(acc, gpost_ref[...])
    xo_ref[...] = y
    if emit_h:
        maybe_h[0][...] = _rms(y, gnext_ref[...]).astype(BF16)


def _ffn(x, g_pre, g_post, g_next, wgu, wdn, emit_h):
    n = x.shape[0]
    tm = min(TOKEN_TILE, n)
    row = pl.BlockSpec((tm, D_MODEL), lambda i: (i, 0))
    out_shape = [jax.ShapeDtypeStruct((n, D_MODEL), F32)]
    out_specs = [row]
    if emit_h:
        out_shape.append(jax.ShapeDtypeStruct((n, D_MODEL), BF16))
        out_specs.append(row)
    res = pl.pallas_call(
        functools.partial(_ffn_kernel, emit_h=emit_h),
        grid=(n // tm,),
        in_specs=[row, _const_spec((1, D_MODEL)), _const_spec((1, D_MODEL)), _const_spec((1, D_MODEL)),
                  _const_spec((D_MODEL, 2 * D_FF)), _const_spec((D_FF, D_MODEL))],
        out_specs=out_specs,
        out_shape=out_shape,
        compiler_params=_cparams(1),
    )(x, g_pre[None], g_post[None], g_next[None], wgu, wdn)
    return res if emit_h else (res[0], None)


def _proj_kernel(h_ref, cos_ref, sin_ref, wq_ref, wkv_ref, wg_ref,
                 q_ref, cmp_ref, sel_ref, win_ref, gate_ref, *prompt_refs):
    h = h_ref[...]
    cos = cos_ref[...]
    sin = sin_ref[...]
    q = _rope_lanes(_dot(h, wq_ref[...]), cos, sin)
    q_ref[...] = (q * Q_SCALE).astype(BF16)
    kv = _dot(h, wkv_ref[...])
    cmp_ref[...] = kv[:, 0:2 * KV_W]
    ks = _rope_lanes(kv[:, 2 * KV_W:3 * KV_W], cos, sin)
    vs = kv[:, 3 * KV_W:4 * KV_W]
    kw = _rope_lanes(kv[:, 4 * KV_W:5 * KV_W], cos, sin)
    vw = kv[:, 5 * KV_W:6 * KV_W]
    sel_ref[:, 0:KV_W] = ks
    sel_ref[:, KV_W:2 * KV_W] = vs
    win_ref[:, 0:KV_W] = kw
    win_ref[:, KV_W:2 * KV_W] = vw
    gate_ref[...] = _sigmoid(_dot(h, wg_ref[...]))
    if prompt_refs:
        kst_ref, vs1_ref, kwt_ref, vw1_ref = prompt_refs
        kst_ref[0] = ks.T.astype(BF16)
        kwt_ref[0] = kw.T.astype(BF16)
        for v, v1_ref in ((vs, vs1_ref), (vw, vw1_ref)):
            for g in range(N_KV):
                pair = v[:, (g // 2) * 128:(g // 2 + 1) * 128]
                if g % 2:
                    pair = pltpu.roll(pair, HEAD_DIM, 1)
                lane = lax.broadcasted_iota(jnp.int32, pair.shape, 1)
                v1_ref[0, g] = jnp.where(lane < HEAD_DIM, pair, 1.0).astype(BF16)


def _proj(h, cos, sin, wq, wkv, wg, batch=None):
    n = h.shape[0]
    tm = min(TOKEN_TILE, n)
    n_pos_tiles = cos.shape[0] // tm
    row = lambda w: pl.BlockSpec((tm, w), lambda i: (i, 0))
    tab = pl.BlockSpec((tm, 128), lambda i: (i % n_pos_tiles, 0))
    out_shape = [jax.ShapeDtypeStruct((n, Q_W), BF16), jax.ShapeDtypeStruct((n, 2 * KV_W), F32),
                 jax.ShapeDtypeStruct((n, 2 * KV_W), F32), jax.ShapeDtypeStruct((n, 2 * KV_W), F32),
                 jax.ShapeDtypeStruct((n, N_GATE), F32)]
    out_specs = [row(Q_W), row(2 * KV_W), row(2 * KV_W), row(2 * KV_W), row(N_GATE)]
    if batch is not None:
        t = n // batch
        nt = t // tm
        kt_spec = pl.BlockSpec((1, KV_W, tm), lambda i: (i // nt, 0, i % nt))
        v1_spec = pl.BlockSpec((1, N_KV, tm, 2 * HEAD_DIM), lambda i: (i // nt, 0, i % nt, 0))
        out_shape += [jax.ShapeDtypeStruct((batch, KV_W, t), BF16),
                      jax.ShapeDtypeStruct((batch, N_KV, t, 2 * HEAD_DIM), BF16)] * 2
        out_specs += [kt_spec, v1_spec] * 2
    return pl.pallas_call(
        _proj_kernel,
        grid=(n // tm,),
        in_specs=[row(D_MODEL), tab, tab, _const_spec((D_MODEL, Q_W)), _const_spec((D_MODEL, 6 * KV_W)),
                  _const_spec((D_MODEL, N_GATE))],
        out_specs=out_specs,
        out_shape=out_shape,
        compiler_params=_cparams(1),
    )(h, cos, sin, wq, wkv, wg)


def _compress_kernel(pt_ref, *refs, npg, nch):
    del pt_ref
    pages = refs[:npg]
    w1_ref, pe_ref, b1_ref, w2_ref, b2_ref, cos_ref, sin_ref, kt_ref, v_ref, part_ref = refs[npg:]
    j = pl.program_id(1)
    rows = npg * (PAGE_SIZE // CMP_STRIDE)
    row0 = pl.multiple_of(j * rows, rows)
    for kv in range(2):
        acc = [jnp.zeros((rows, KV_W), F32) for _ in range(CMP_LEN // CMP_STRIDE)]
        for i in range(CMP_STRIDE):
            lo = i * 2 * KV_W + kv * KV_W
            x = jnp.concatenate([p[0, 0, :, lo:lo + KV_W] for p in pages], axis=0)
            for r in range(CMP_LEN // CMP_STRIDE):
                xr = (x + pe_ref[kv, r, i:i + 1, :]).astype(BF16)
                acc[r] = acc[r] + _dot(xr, w1_ref[kv, r, i])
        for r in range(CMP_LEN // CMP_STRIDE):
            part_ref[kv, pl.ds(row0, rows), r * KV_W:(r + 1) * KV_W] = acc[r]

    @pl.when(j == pl.num_programs(1) - 1)
    def _():
        for kv in range(2):
            hid = b1_ref[kv:kv + 1, :] + part_ref[kv, :, 0:KV_W]
            hid = hid + pltpu.roll(part_ref[kv, :, KV_W:2 * KV_W], nch - 1, 0)
            act = jax.nn.gelu(hid, approximate=True).astype(BF16)
            out = _dot(act, w2_ref[kv]) + b2_ref[kv:kv + 1, :]
            rowi = lax.broadcasted_iota(jnp.int32, out.shape, 0)
            out = jnp.where(rowi < nch - 1, out, 0.0)
            if kv == 0:
                out = _rope_lanes(out, cos_ref[...], sin_ref[...])
                kt_ref[0] = out.T.astype(BF16)
            else:
                v_ref[0] = out.astype(BF16)


def _page_index(s, j, pt, *, k, npg, layer, n_trail):
    return (layer, pt[s, j * npg + k]) + (0,) * n_trail


def _chunk_view(pool):
    return pool.reshape(pool.shape[0], pool.shape[1], PAGE_SIZE // CMP_STRIDE, CMP_STRIDE * 2 * KV_W)


def _compress(pool8, layer, page_table, cw):
    s_n, p_n = page_table.shape
    chunks_per_page = PAGE_SIZE // CMP_STRIDE
    nch = p_n * chunks_per_page
    npg = min(CMP_PAGES_PER_STEP, p_n)
    row_w = CMP_STRIDE * 2 * KV_W
    page_specs = [pl.BlockSpec((1, 1, chunks_per_page, row_w),
                               functools.partial(_page_index, k=k, npg=npg, layer=layer, n_trail=2))
                  for k in range(npg)]
    consts = [cw["w1"], cw["pe"], cw["b1"], cw["w2"], cw["b2"], cw["cos"], cw["sin"]]
    grid_spec = pltpu.PrefetchScalarGridSpec(
        num_scalar_prefetch=1,
        grid=(s_n, p_n // npg),
        in_specs=page_specs + [_const_spec(c.shape) for c in consts],
        out_specs=[pl.BlockSpec((1, KV_W, nch), lambda s, j, pt: (s, 0, 0)),
                   pl.BlockSpec((1, nch, KV_W), lambda s, j, pt: (s, 0, 0))],
        scratch_shapes=[pltpu.VMEM((2, nch, 2 * KV_W), F32)],
    )
    return pl.pallas_call(
        functools.partial(_compress_kernel, npg=npg, nch=nch),
        grid_spec=grid_spec,
        out_shape=[jax.ShapeDtypeStruct((s_n, KV_W, nch), BF16), jax.ShapeDtypeStruct((s_n, nch, KV_W), BF16)],
        compiler_params=_cparams(2),
    )(page_table, *([pool8] * npg), *consts)


def _rank_values(imp_t, valid, forced):
    bits = pltpu.bitcast(imp_t, jnp.int32)
    return jnp.where(forced, FORCED_KEY, jnp.where(valid, bits, -1))


def _rank_counts(v_scr, groups, n_rank_rows):
    shape = v_scr.shape[1:]
    n_idx = lax.broadcasted_iota(jnp.int32, shape, 0)

    def body(m, cnts):
        later = jnp.where(n_idx > m, 1, 0)
        out = []
        for g, cnt in zip(groups, cnts):
            row = v_scr[g, pl.ds(m, 1), :]
            out.append(cnt + jnp.where(row > v_scr[g] - later, 1, 0))
        return tuple(out)

    return lax.fori_loop(0, n_rank_rows, body, tuple(jnp.zeros(shape, jnp.int32) for _ in groups))


def _importance_t(p_sum, ovt_ref):
    hi = p_sum.astype(BF16)
    lo = (p_sum - hi.astype(F32)).astype(BF16)
    return _dot_nt(ovt_ref[...], hi) + _dot_nt(ovt_ref[...], lo)


SELNEG_LANES = MAX_SEL_BLOCKS
Q_LANE0 = SELNEG_LANES


def _key_operand(mask_rows, k_rows):
    n = k_rows.shape[1]
    top = jnp.zeros((SELNEG_LANES, n), BF16) if mask_rows is None else mask_rows
    return jnp.concatenate([top, k_rows, jnp.zeros((KV_W - SELNEG_LANES - HEAD_DIM, n), BF16)], axis=0)


def _ones_normalise(acc):
    lane = lax.broadcasted_iota(jnp.int32, acc.shape, 1)
    return jnp.where(lane < HEAD_DIM, acc * (1.0 / pltpu.roll(acc, HEAD_DIM, 1)), 0.0)


def _biased(s, bias):
    n = s.shape[1]
    return (s.reshape(GROUP, Q_BLOCK, n) + bias[None]).reshape(s.shape)


def _attn_prompt_kernel(q_ref, gate_ref, x_ref, kcct_ref, vcc_ref, kst_ref, vs1_ref, kwt_ref, vw1_ref,
                        eneg_ref, ovt_ref, wout_ref, g3_ref, o_ref, qm_scr, oc_scr, v_scr, acc_scr):
    qb = pl.program_id(1)
    q0 = qb * Q_BLOCK
    rows = GROUP * Q_BLOCK
    nch = kcct_ref.shape[2]
    heads = lambda g: slice(g * HEAD_DIM, (g + 1) * HEAD_DIM)
    tok_pos = lambda n: q0 + lax.broadcasted_iota(jnp.int32, (Q_BLOCK, n), 0)
    key_idx = lambda n: lax.broadcasted_iota(jnp.int32, (Q_BLOCK, n), 1)

    c_bias = jnp.where(key_idx(nch) * CMP_STRIDE + (CMP_LEN - 1) <= tok_pos(nch), 0.0, NEG)
    n_idx = lax.broadcasted_iota(jnp.int32, (MAX_SEL_BLOCKS, Q_BLOCK), 0)
    cur = (q0 + lax.broadcasted_iota(jnp.int32, (MAX_SEL_BLOCKS, Q_BLOCK), 1)) // SEL_BLOCK
    valid = n_idx <= cur
    forced = valid & ((n_idx == 0) | (n_idx >= cur - 1))
    for g in range(N_KV):
        qg = q_ref[0, :, g * KV_W:(g + 1) * KV_W].astype(F32)
        q_lanes = _head_lane_mask(qg.shape, Q_LANE0 // HEAD_DIM)
        qm = jnp.concatenate(
            [jnp.where(q_lanes, _lane_place(qg, Q_LANE0 - j * HEAD_DIM), 0.0) for j in range(GROUP)], axis=0
        ).astype(BF16)
        qm_scr[g] = qm
        s_c = _biased(_dot(qm, _key_operand(None, kcct_ref[0, heads(g), :])), c_bias)
        m_c = jnp.max(s_c, axis=-1, keepdims=True)
        p_c = jnp.exp2(s_c - m_c)
        d_c = jnp.sum(p_c, axis=-1, keepdims=True)
        p_c = p_c * jnp.where(m_c > 0.5 * NEG, 1.0 / d_c, 0.0)
        o_c = _dot(p_c.astype(BF16), vcc_ref[0])
        o_c = o_c[:, (g // 2) * 128:(g // 2 + 1) * 128]
        oc_scr[g] = pltpu.roll(o_c, HEAD_DIM, 1) if g % 2 else o_c
        p_sum = p_c[0:Q_BLOCK]
        for j in range(1, GROUP):
            p_sum = p_sum + p_c[j * Q_BLOCK:(j + 1) * Q_BLOCK]
        v_scr[g] = _rank_values(_importance_t(p_sum, ovt_ref), valid, forced)

    n_live = jnp.minimum((q0 + Q_BLOCK) // SEL_BLOCK, MAX_SEL_BLOCKS)
    for pair in ((0, 1), (2, 3)):
        cnts = _rank_counts(v_scr, pair, n_live)
        for g, cnt in zip(pair, cnts):
            not_picked = jnp.where(valid & (cnt < SEL_TOP), 0.0, 1.0).T.astype(BF16)
            for j in range(GROUP):
                qm_scr[g, j * Q_BLOCK:(j + 1) * Q_BLOCK, 0:SELNEG_LANES] = not_picked

    def sel_tile(kt, ms):
        k0 = pl.multiple_of(kt * KEY_TILE, KEY_TILE)
        blk_bias = eneg_ref[:, pl.ds(k0, KEY_TILE)]
        tri = jnp.where(k0 + key_idx(KEY_TILE) <= tok_pos(KEY_TILE), 0.0, NEG)
        out = []
        for g in range(N_KV):
            s = _dot(qm_scr[g], _key_operand(blk_bias, kst_ref[0, heads(g), pl.ds(k0, KEY_TILE)]))
            s = _biased(s, tri)
            m_new = jnp.maximum(ms[g], jnp.max(s, axis=-1, keepdims=True))
            alpha = jnp.exp2(ms[g] - m_new)
            p = jnp.exp2(s - m_new).astype(BF16)
            acc_scr[g] = alpha * acc_scr[g] + _dot(p, vs1_ref[0, g, pl.ds(k0, KEY_TILE), :])
            out.append(m_new)
        return tuple(out)

    acc_scr[...] = jnp.zeros(acc_scr.shape, F32)
    n_tiles = (q0 + Q_BLOCK + KEY_TILE - 1) // KEY_TILE
    lax.fori_loop(0, n_tiles, sel_tile, tuple(jnp.full((rows, 1), NEG, F32) for _ in range(N_KV)))

    wlen = WINDOW + Q_BLOCK
    w0 = pl.multiple_of(jnp.maximum(q0 - WINDOW, 0), Q_BLOCK)
    w_key = w0 + key_idx(wlen)
    w_bias = jnp.where((w_key <= tok_pos(wlen)) & (w_key > tok_pos(wlen) - WINDOW), 0.0, NEG)
    slabs = []
    for g in range(N_KV):
        s_w = _biased(_dot(qm_scr[g], _key_operand(None, kwt_ref[0, heads(g), pl.ds(w0, wlen)])), w_bias)
        p_w = jnp.exp2(s_w - jnp.max(s_w, axis=-1, keepdims=True)).astype(BF16)
        o_w = _ones_normalise(_dot(p_w, vw1_ref[0, g, pl.ds(w0, wlen), :]))
        o_s = _ones_normalise(acc_scr[g])
        o_c = oc_scr[g]
        halves = [jnp.zeros((Q_BLOCK, 2 * HEAD_DIM), F32) for _ in range(GROUP // 2)]
        for j in range(GROUP):
            c = (g * GROUP + j) * 3
            rs = slice(j * Q_BLOCK, (j + 1) * Q_BLOCK)
            mix = (gate_ref[0, :, c:c + 1] * o_c[rs] + gate_ref[0, :, c + 1:c + 2] * o_s[rs]
                   + gate_ref[0, :, c + 2:c + 3] * o_w[rs])
            mix = jnp.where(_head_lane_mask(mix.shape, j % 2), _lane_place(mix, (j % 2) * HEAD_DIM), 0.0)
            halves[j // 2] = halves[j // 2] + mix
        slabs += halves
    o = jnp.concatenate(slabs, axis=1).astype(BF16)
    o_ref[0] = x_ref[0] + _rms(_dot(o, wout_ref[...]), g3_ref[...])


def _attn_prompt(q, gates, x, kcct, vcc, kst, vs1, kwt, vw1, eneg, ovt, wout, g3):
    b, t, _ = q.shape
    nch = kcct.shape[2]
    rows = GROUP * Q_BLOCK
    per_q = lambda w: pl.BlockSpec((1, Q_BLOCK, w), lambda bi, qi: (bi, qi, 0))
    per_b = lambda s: pl.BlockSpec((1,) + s, lambda bi, qi: (bi,) + (0,) * len(s), pipeline_mode=pl.Buffered(1))
    return pl.pallas_call(
        _attn_prompt_kernel,
        grid=(b, t // Q_BLOCK),
        in_specs=[per_q(Q_W), per_q(N_GATE), per_q(D_MODEL),
                  per_b((KV_W, nch)), per_b((nch, KV_W)),
                  per_b((KV_W, t)), per_b((N_KV, t, 2 * HEAD_DIM)), per_b((KV_W, t)), per_b((N_KV, t, 2 * HEAD_DIM)),
                  _const_spec(eneg.shape), _const_spec(ovt.shape), _const_spec((Q_W, D_MODEL)),
                  _const_spec((1, D_MODEL))],
        out_specs=per_q(D_MODEL),
        out_shape=jax.ShapeDtypeStruct((b, t, D_MODEL), F32),
        scratch_shapes=[pltpu.VMEM((N_KV, rows, KV_W), BF16), pltpu.VMEM((N_KV, rows, 2 * HEAD_DIM), F32),
                        pltpu.VMEM((N_KV, MAX_SEL_BLOCKS, Q_BLOCK), jnp.int32),
                        pltpu.VMEM((N_KV, rows, 2 * HEAD_DIM), F32)],
        compiler_params=_cparams(2),
    )(q, gates, x, kcct, vcc, kst, vs1, kwt, vw1, eneg, ovt, wout, g3[None])


TOK_PAD = 8


def _attn_sample_kernel(pt_ref, q_ref, gate_ref, x_ref, kcct_ref, vcc_ref, selnew_ref, winnew_ref, winbuf_ref,
                        *refs, npg, n_tok, past):
    del pt_ref
    pages = refs[:npg]
    (e_ref, ovt_ref, wout_ref, g3_ref, o_ref,
     qexp_scr, selrow_scr, m_scr, l_scr, acc_scr, oc_scr, v_scr) = refs[npg:]
    k = pl.program_id(1)
    rows = N_HEADS * TOK_PAD
    n_cache_blocks = past // SEL_BLOCK

    def row_tok(shape):
        return lax.broadcasted_iota(jnp.int32, shape, 0) % TOK_PAD

    def pad_rows(a, n):
        return jnp.concatenate([a, jnp.zeros((n - a.shape[0], a.shape[1]), a.dtype)], axis=0)

    @pl.when(k == 0)
    def _():
        q8 = q_ref[0].astype(F32)
        pieces = []
        for g in range(N_KV):
            qg = q8[:, g * KV_W:(g + 1) * KV_W]
            g_lanes = _head_lane_mask(qg.shape, g)
            for j in range(GROUP):
                pieces.append(jnp.where(g_lanes, _lane_place(qg, (g - j) * HEAD_DIM), 0.0))
        qexp = jnp.concatenate(pieces, axis=0).astype(BF16)
        qexp_scr[...] = qexp

        s_c = _dot(qexp, kcct_ref[0])
        c_end = lax.broadcasted_iota(jnp.int32, s_c.shape, 1) * CMP_STRIDE + (CMP_LEN - 1)
        p_c = _masked_softmax2(s_c, c_end <= past + row_tok(s_c.shape))
        oc_scr[...] = _dot(p_c.astype(BF16), vcc_ref[0])

        sums = []
        for g in range(N_KV):
            acc = p_c[(g * GROUP) * TOK_PAD:(g * GROUP + 1) * TOK_PAD]
            for j in range(1, GROUP):
                acc = acc + p_c[(g * GROUP + j) * TOK_PAD:(g * GROUP + j + 1) * TOK_PAD]
            sums.append(acc)
        p_sum = pad_rows(jnp.concatenate(sums, axis=0), MAX_SEL_BLOCKS)
        imp_t = _importance_t(p_sum, ovt_ref)
        n_idx = lax.broadcasted_iota(jnp.int32, imp_t.shape, 0)
        cur = (past + lax.broadcasted_iota(jnp.int32, imp_t.shape, 1) % TOK_PAD) // SEL_BLOCK
        valid = n_idx <= cur
        forced = valid & ((n_idx == 0) | (n_idx >= cur - 1))
        v_scr[0] = _rank_values(imp_t, valid, forced)
        (cnt,) = _rank_counts(v_scr, (0,), n_cache_blocks)
        sel = jnp.where(valid & (cnt < SEL_TOP - 1), 1.0, 0.0).T
        sel_rows = []
        for g in range(N_KV):
            sel_rows += [sel[g * TOK_PAD:(g + 1) * TOK_PAD]] * GROUP
        selrow_scr[...] = jnp.concatenate(sel_rows, axis=0).astype(BF16)
        m_scr[...] = jnp.full(m_scr.shape, NEG, F32)
        l_scr[...] = jnp.zeros(l_scr.shape, F32)
        acc_scr[...] = jnp.zeros(acc_scr.shape, F32)

    def online_update(s, v_tiles):
        m_old = m_scr[...]
        m_new = jnp.maximum(m_old, jnp.max(s, axis=-1, keepdims=True))
        alpha = jnp.exp2(m_old - m_new)
        p = jnp.exp2(s - m_new)
        l_scr[...] = alpha * l_scr[...] + jnp.sum(p, axis=-1, keepdims=True)
        acc = alpha * acc_scr[...]
        w = s.shape[1] // len(v_tiles)
        for i, (v, transposed) in enumerate(v_tiles):
            pv = (_dot_nt if transposed else _dot)
            acc = acc + pv(p[:, i * w:(i + 1) * w].astype(BF16), v)
        acc_scr[...] = acc
        m_scr[...] = m_new

    page_t = lambda p, kv: p[0, 0, kv].reshape(KV_W, PAGE_SIZE).astype(BF16)
    qexp = qexp_scr[...]
    keys = npg * PAGE_SIZE
    k0 = pl.multiple_of(k * keys, keys)
    s = jnp.concatenate([_dot(qexp, page_t(p, 0)) for p in pages], axis=1)
    picked = _dot(selrow_scr[...], e_ref[:, pl.ds(k0, keys)])
    s = s + jnp.where(picked > 0.5, 0.0, NEG)
    online_update(s, [(page_t(p, 1), True) for p in pages])

    @pl.when(k == pl.num_programs(1) - 1)
    def _():
        tok = row_tok((rows, PAGE_SIZE))
        new_i = lax.broadcasted_iota(jnp.int32, (rows, PAGE_SIZE), 1)
        new_ok = (new_i < n_tok) & (new_i <= tok)

        sel_new = pad_rows(selnew_ref[0], PAGE_SIZE)
        s_n = _dot_nt(qexp, sel_new[:, 0:KV_W].astype(BF16))
        online_update(jnp.where(new_ok, s_n, NEG), [(sel_new[:, KV_W:2 * KV_W].astype(BF16), False)])
        o_s = acc_scr[...] * (1.0 / l_scr[...])

        nbuf = winbuf_ref.shape[-1]
        buf_t = lambda kv: winbuf_ref[0, 0, kv].reshape(KV_W, nbuf).astype(BF16)
        win_new = pad_rows(winnew_ref[0], PAGE_SIZE)
        s_b = _dot(qexp, buf_t(0))
        s_nw = _dot_nt(qexp, win_new[:, 0:KV_W].astype(BF16))
        s_w = jnp.concatenate([s_b, s_nw], axis=1)
        col = lax.broadcasted_iota(jnp.int32, s_w.shape, 1)
        tok_w = row_tok(s_w.shape)
        in_buf = (col < nbuf) & (past - nbuf + col > past + tok_w - WINDOW) & (past - nbuf + col >= 0)
        in_new = (col >= nbuf) & (col - nbuf < n_tok) & (col - nbuf <= tok_w)
        p_w = _masked_softmax2(s_w, in_buf | in_new)
        o_w = (_dot_nt(p_w[:, 0:nbuf].astype(BF16), buf_t(1))
               + _dot(p_w[:, nbuf:].astype(BF16), win_new[:, KV_W:2 * KV_W].astype(BF16)))

        o_c = oc_scr[...]
        gates = gate_ref[0]
        slabs = []
        for g in range(N_KV):
            slab = jnp.zeros((TOK_PAD, KV_W), F32)
            for j in range(GROUP):
                c = (g * GROUP + j) * 3
                rs = slice((g * GROUP + j) * TOK_PAD, (g * GROUP + j + 1) * TOK_PAD)
                mix = gates[:, c:c + 1] * o_c[rs] + gates[:, c + 1:c + 2] * o_s[rs] + gates[:, c + 2:c + 3] * o_w[rs]
                slab = slab + jnp.where(_head_lane_mask(mix.shape, j), _lane_place(mix, (j - g) * HEAD_DIM), 0.0)
            slabs.append(slab)
        o = jnp.concatenate(slabs, axis=1).astype(BF16)
        y = _rms(_dot(o, wout_ref[...]), g3_ref[...])
        o_ref[0] = x_ref[0] + y


def _attn_sample(q, gates, x, kcct, vcc, sel_new, win_new, win_buf_t, sel_pool_t, layer, page_table, e_mat, ovt,
                 wout, g3, past):
    s_n, n_tok, _ = q.shape
    pad_tok = lambda a: jnp.pad(a, ((0, 0), (0, TOK_PAD - n_tok), (0, 0)))
    q, gates, x, sel_new, win_new = (pad_tok(a) for a in (q, gates, x, sel_new, win_new))
    p_n = page_table.shape[1]
    npg = min(SEL_PAGES_PER_STEP, p_n)
    nch = kcct.shape[2]
    nbuf = win_buf_t.shape[-1]
    rows = N_HEADS * TOK_PAD
    per_s = lambda shp: pl.BlockSpec((1,) + shp, lambda s, k, pt: (s, 0, 0))
    page_specs = [pl.BlockSpec((1, 1, 2, N_KV, HEAD_DIM, PAGE_SIZE),
                               functools.partial(_page_index, k=i, npg=npg, layer=layer, n_trail=4))
                  for i in range(npg)]
    buf_spec = pl.BlockSpec((1, 1, 2, N_KV, HEAD_DIM, nbuf), lambda s, k, pt: (layer, s, 0, 0, 0, 0))
    grid_spec = pltpu.PrefetchScalarGridSpec(
        num_scalar_prefetch=1,
        grid=(s_n, p_n // npg),
        in_specs=[per_s((TOK_PAD, Q_W)), per_s((TOK_PAD, N_GATE)), per_s((TOK_PAD, D_MODEL)),
                  per_s((KV_W, nch)), per_s((nch, KV_W)), per_s((TOK_PAD, 2 * KV_W)), per_s((TOK_PAD, 2 * KV_W)),
                  buf_spec] + page_specs +
                 [_const_spec(e_mat.shape), _const_spec(ovt.shape), _const_spec((Q_W, D_MODEL)),
                  _const_spec((1, D_MODEL))],
        out_specs=per_s((TOK_PAD, D_MODEL)),
        scratch_shapes=[pltpu.VMEM((rows, KV_W), BF16), pltpu.VMEM((rows, MAX_SEL_BLOCKS), BF16),
                        pltpu.VMEM((rows, 1), F32), pltpu.VMEM((rows, 1), F32), pltpu.VMEM((rows, KV_W), F32),
                        pltpu.VMEM((rows, KV_W), F32), pltpu.VMEM((1, MAX_SEL_BLOCKS, MAX_SEL_BLOCKS), jnp.int32)],
    )
    return pl.pallas_call(
        functools.partial(_attn_sample_kernel, npg=npg, n_tok=n_tok, past=past),
        grid_spec=grid_spec,
        out_shape=jax.ShapeDtypeStruct((s_n, TOK_PAD, D_MODEL), F32),
        compiler_params=_cparams(2),
    )(page_table, q, gates, x, kcct, vcc, sel_new, win_new, win_buf_t, *([sel_pool_t] * npg), e_mat, ovt, wout,
      g3[None])[:, :n_tok]


def _conv_tail(y, lng_ref, lnb_ref, w2_ref, b2_ref, g3_ref):
    mean = jnp.mean(y, axis=-1, keepdims=True)
    yc = y - mean
    yn = yc * lax.rsqrt(jnp.mean(yc * yc, axis=-1, keepdims=True) + EPS) * lng_ref[...] + lnb_ref[...]
    act = (yn * _sigmoid(yn)).astype(BF16)
    return _rms(_dot(act, w2_ref[...]) + b2_ref[...], g3_ref[...])


def _glu(h, w1_ref, b1_ref):
    ag = _dot(h, w1_ref[...]) + b1_ref[...]
    d = ag.shape[1] // 2
    return ag[:, 0:d] * _sigmoid(ag[:, d:])


def _conv_prompt_kernel(h_ref, x_ref, w1_ref, b1_ref, wdw_ref, bdw_ref, lng_ref, lnb_ref, w2_ref, b2_ref, g3_ref,
                        o_ref, st_ref, ctx_scr):
    ti = pl.program_id(1)
    tm = h_ref.shape[1]

    @pl.when(ti == 0)
    def _():
        ctx_scr[0:CONV_HALO, :] = jnp.zeros((CONV_HALO, ctx_scr.shape[1]), F32)

    ctx_scr[CONV_HALO:CONV_HALO + tm, :] = _glu(h_ref[0], w1_ref, b1_ref)
    off = CONV_HALO - (CONV_WIDTH - 1)
    y = jnp.zeros((tm, ctx_scr.shape[1]), F32)
    for kk in range(CONV_WIDTH):
        y = y + ctx_scr[off + kk:off + kk + tm, :] * wdw_ref[kk:kk + 1, :]
    y = y + bdw_ref[...]
    o_ref[0] = x_ref[0] + _conv_tail(y, lng_ref, lnb_ref, w2_ref, b2_ref, g3_ref)

    @pl.when(ti == pl.num_programs(1) - 1)
    def _():
        st_ref[0] = ctx_scr[tm + off:tm + CONV_HALO, :]

    ctx_scr[0:CONV_HALO, :] = ctx_scr[tm:tm + CONV_HALO, :]


def _conv_prompt(h, x, cv, g3):
    b, t, _ = h.shape
    d_in = cv["wdw"].shape[1]
    tm = min(TOKEN_TILE, t)
    row = pl.BlockSpec((1, tm, D_MODEL), lambda bi, ti: (bi, ti, 0))
    consts = [cv["w1"], cv["b1"], cv["wdw"], cv["bdw"], cv["lng"], cv["lnb"], cv["w2"], cv["b2"], g3[None]]
    return pl.pallas_call(
        _conv_prompt_kernel,
        grid=(b, t // tm),
        in_specs=[row, row] + [_const_spec(c.shape) for c in consts],
        out_specs=[row, pl.BlockSpec((1, CONV_WIDTH - 1, d_in), lambda bi, ti: (bi, 0, 0))],
        out_shape=[jax.ShapeDtypeStruct((b, t, D_MODEL), F32),
                   jax.ShapeDtypeStruct((b, CONV_WIDTH - 1, d_in), F32)],
        scratch_shapes=[pltpu.VMEM((tm + CONV_HALO, d_in), F32)],
        compiler_params=_cparams(2),
    )(h, x, *consts)


def _conv_sample_kernel(h_ref, x_ref, st_ref, w1_ref, b1_ref, wdw_ref, bdw_ref, lng_ref, lnb_ref, w2_ref, b2_ref,
                        g3_ref, o_ref, sto_ref, *, n_tok, n_seq):
    u = _glu(h_ref[...], w1_ref, b1_ref)
    n_state = CONV_WIDTH - 1

    def ctx(i):
        return st_ref[i] if i < n_state else u[(i - n_state) * n_seq:(i - n_state + 1) * n_seq]

    ys = []
    for t in range(n_tok):
        y = jnp.zeros((n_seq, u.shape[1]), F32)
        for kk in range(CONV_WIDTH):
            y = y + ctx(t + kk) * wdw_ref[kk:kk + 1, :]
        ys.append(y)
    y = jnp.concatenate(ys, axis=0) + bdw_ref[...]
    o_ref[...] = x_ref[...] + _conv_tail(y, lng_ref, lnb_ref, w2_ref, b2_ref, g3_ref)
    for i in range(n_state):
        sto_ref[i] = ctx(i + n_tok)


def _conv_sample(h_ts, x_ts, state_t, cv, g3, n_tok):
    n, _ = h_ts.shape
    n_seq = n // n_tok
    d_in = cv["wdw"].shape[1]
    args = [h_ts, x_ts, state_t, cv["w1"], cv["b1"], cv["wdw"], cv["bdw"], cv["lng"], cv["lnb"], cv["w2"],
            cv["b2"], g3[None]]
    return pl.pallas_call(
        functools.partial(_conv_sample_kernel, n_tok=n_tok, n_seq=n_seq),
        grid=(1,),
        in_specs=[_const_spec(a.shape) for a in args],
        out_specs=[pl.BlockSpec((n, D_MODEL), lambda i: (0, 0)),
                   pl.BlockSpec((CONV_WIDTH - 1, n_seq, d_in), lambda i: (0, 0, 0))],
        out_shape=[jax.ShapeDtypeStruct((n, D_MODEL), F32),
                   jax.ShapeDtypeStruct((CONV_WIDTH - 1, n_seq, d_in), F32)],
        compiler_params=_cparams(1),
    )(*args)


def _rope_tables(pos):
    half = HEAD_DIM // 2
    inv = ROPE_THETA ** (-jnp.arange(half, dtype=F32) / half)
    ang = pos.astype(F32)[:, None] * inv[None, :]
    cos, sin = jnp.cos(ang), jnp.sin(ang)
    return jnp.concatenate([cos, cos, cos, cos], axis=1), jnp.concatenate([-sin, sin, -sin, sin], axis=1)


def _block_expand_matrix(n_keys):
    n = np.arange(MAX_SEL_BLOCKS)[:, None]
    k = np.arange(n_keys)[None, :]
    return jnp.asarray((k // SEL_BLOCK == n).astype(np.float32), dtype=BF16)


def _overlap_t(nch):
    n = np.arange(MAX_SEL_BLOCKS)[:, None] * SEL_BLOCK
    c = np.arange(nch)[None, :] * CMP_STRIDE
    ov = (c < n + SEL_BLOCK) & (c + CMP_LEN > n) & (np.arange(nch)[None, :] < nch - 1)
    return jnp.asarray(ov.astype(np.float32), dtype=BF16)


def _compress_weights(pe, w1, b1, w2, b2, nch):
    ratio = CMP_LEN // CMP_STRIDE
    eye = jnp.eye(N_KV, dtype=F32)
    w1r = w1.reshape(2, ratio, CMP_STRIDE, HEAD_DIM, HEAD_DIM)
    w1bd = jnp.einsum("gh,kride->krigdhe", eye, w1r).reshape(2, ratio, CMP_STRIDE, KV_W, KV_W).astype(BF16)
    w2bd = jnp.einsum("gh,kde->kgdhe", eye, w2).reshape(2, KV_W, KV_W).astype(BF16)
    cos, sin = _rope_tables(jnp.arange(nch) * CMP_STRIDE + CMP_LEN - 1)
    return {
        "w1": w1bd,
        "pe": jnp.tile(pe.reshape(2, ratio, CMP_STRIDE, HEAD_DIM), (1, 1, 1, N_KV)),
        "b1": jnp.tile(b1, (1, N_KV)),
        "w2": w2bd,
        "b2": jnp.tile(b2, (1, N_KV)),
        "cos": cos,
        "sin": sin,
    }


def _block_bias_matrix(n_keys):
    n = np.arange(MAX_SEL_BLOCKS)[:, None]
    k = np.arange(n_keys)[None, :]
    return jnp.asarray(np.where(k // SEL_BLOCK == n, NEG, 0.0).astype(np.float32), dtype=BF16)


def _rows_minor(a):
    nd = a.ndim
    return a.transpose(tuple(range(nd - 4)) + (nd - 3, nd - 2, nd - 1, nd - 4))


def _nsa_layer(hp, xp, hs, xs, cmp_pool8, sel_pool_t, win_buf_t, win_buf, layer, page_table, w_in, w_out, pe, w1, b1,
               w2, b2, g3):
    b, t, _ = xp.shape
    s_n, n_tok, _ = xs.shape
    past = page_table.shape[1] * PAGE_SIZE
    assert t % TOKEN_TILE == 0 and t // SEL_BLOCK <= MAX_SEL_BLOCKS and t >= WINDOW + Q_BLOCK
    assert past // SEL_BLOCK == MAX_SEL_BLOCKS and n_tok <= TOK_PAD and past % SEL_BLOCK == 0
    wq = w_in[:, :Q_W].astype(BF16)
    wkv = w_in[:, Q_W:Q_W + 6 * KV_W].astype(BF16)
    wg = w_in[:, Q_W + 6 * KV_W:].astype(BF16)
    wout = w_out.astype(BF16)
    nch_p = t // CMP_STRIDE
    cw = _compress_weights(pe, w1, b1, w2, b2, nch_p)

    cos_p, sin_p = _rope_tables(jnp.arange(t))
    q, cmp_kv, sel_kv, win_kv, gates, kst, vsb, kwt, vwb = _proj(hp.reshape(b * t, D_MODEL), cos_p, sin_p, wq, wkv,
                                                                  wg, batch=b)
    ident_pages = jnp.arange(b * t // PAGE_SIZE, dtype=jnp.int32).reshape(b, t // PAGE_SIZE)
    kcct, vcc = _compress(_chunk_view(cmp_kv.reshape(1, b * t // PAGE_SIZE, PAGE_SIZE, 2 * KV_W)), 0, ident_pages, cw)
    xp_new = _attn_prompt(q.reshape(b, t, Q_W), gates.reshape(b, t, N_GATE), xp, kcct, vcc, kst, vsb, kwt, vwb,
                          _block_bias_matrix(t), _overlap_t(nch_p), wout, g3)
    kv6 = lambda a, lead: a.reshape(lead + (2, N_KV, HEAD_DIM))
    n_win = min(WINDOW, t)
    outs_p = (kv6(cmp_kv, (b, t)), kv6(sel_kv, (b, t)), kv6(win_kv, (b, t))[:, t - n_win:])

    nch_s = past // CMP_STRIDE
    cw_s = cw if nch_s == nch_p else _compress_weights(pe, w1, b1, w2, b2, nch_s)
    cos_s, sin_s = _rope_tables(past + jnp.arange(s_n * n_tok) % n_tok)
    q_s, cmp_s, sel_s, win_s, gates_s = _proj(hs.reshape(s_n * n_tok, D_MODEL), cos_s, sin_s, wq, wkv, wg)
    kcct_s, vcc_s = _compress(cmp_pool8, layer, page_table, cw_s)
    per_seq = lambda a: a.reshape(s_n, n_tok, a.shape[-1])
    nbuf = win_buf.shape[1]
    xs_new = _attn_sample(per_seq(q_s), per_seq(gates_s), xs, kcct_s, vcc_s, per_seq(sel_s), per_seq(win_s),
                          win_buf_t, sel_pool_t, layer, page_table, _block_expand_matrix(past), _overlap_t(nch_s),
                          wout, g3, past)
    win_all = jnp.concatenate([win_buf, kv6(win_s, (s_n, n_tok))], axis=1)[:, n_tok:]
    outs_s = (kv6(cmp_s, (s_n, n_tok)), kv6(sel_s, (s_n, n_tok)), win_all)
    return xp_new, xs_new, outs_p, outs_s


def _conv_layer(hp, xp, hs, xs, state, w_pw1, b_pw1, w_dw, b_dw, ln_g, ln_b, w_pw2, b_pw2, g3):
    s_n, n_tok, _ = xs.shape
    cv = {"w1": w_pw1.astype(BF16), "b1": b_pw1[None], "wdw": w_dw, "bdw": b_dw[None], "lng": ln_g[None],
          "lnb": ln_b[None], "w2": w_pw2.astype(BF16), "b2": b_pw2[None]}
    xp_new, st_p = _conv_prompt(hp, xp, cv, g3)
    to_ts = lambda a: a.reshape(s_n, n_tok, -1).transpose(1, 0, 2).reshape(s_n * n_tok, -1)
    xs_ts, st_t = _conv_sample(to_ts(hs), to_ts(xs), state.transpose(1, 0, 2), cv, g3, n_tok)
    xs_new = xs_ts.reshape(n_tok, s_n, -1).transpose(1, 0, 2)
    return xp_new, xs_new, st_p, st_t.transpose(1, 0, 2)


def kernel(x_prompt, x_sample, cache_cmp_kv, cache_sel_kv, state_win_kv, state_conv, page_table, norm_g, ffn_w_gu,
           ffn_w_down, attn_w_in, attn_w_out, cmp_pe, cmp_w1, cmp_b1, cmp_w2, cmp_b2, conv_w_pw1, conv_b_pw1,
           conv_w_dw, conv_b_dw, conv_ln_g, conv_ln_b, conv_w_pw2, conv_b_pw2):
    b, t, d = x_prompt.shape
    s_n, n_tok, _ = x_sample.shape
    depth = norm_g.shape[0]
    xp = x_prompt.reshape(b * t, d)
    xs = x_sample.reshape(s_n * n_tok, d)
    cmp_p, sel_p, win_p, conv_p = [], [], [], []
    cmp_s, sel_s, win_s, conv_s = [], [], [], []
    cmp_pool8 = _chunk_view(cache_cmp_kv)
    sel_pool_t = _rows_minor(cache_sel_kv)
    win_buf_t = _rows_minor(state_win_kv)
    for i in range(depth):
        g = norm_g[i]
        wgu, wdn = ffn_w_gu[i, 0].astype(BF16), ffn_w_down[i, 0].astype(BF16)
        xp, hp = _ffn(xp, g[0], g[1], g[2], wgu, wdn, True)
        xs, hs = _ffn(xs, g[0], g[1], g[2], wgu, wdn, True)
        j = i // 2
        xp3, xs3 = xp.reshape(b, t, d), xs.reshape(s_n, n_tok, d)
        hp3, hs3 = hp.reshape(b, t, d), hs.reshape(s_n, n_tok, d)
        if i % 2 == 0:
            xp3, xs3, o_p, o_s = _nsa_layer(hp3, xp3, hs3, xs3, cmp_pool8, sel_pool_t, win_buf_t, state_win_kv[j], j,
                                            page_table, attn_w_in[j], attn_w_out[j], cmp_pe[j], cmp_w1[j], cmp_b1[j],
                                            cmp_w2[j], cmp_b2[j], g[3])
            cmp_p.append(o_p[0]); sel_p.append(o_p[1]); win_p.append(o_p[2])
            cmp_s.append(o_s[0]); sel_s.append(o_s[1]); win_s.append(o_s[2])
        else:
            xp3, xs3, st_p, st_s = _conv_layer(hp3, xp3, hs3, xs3, state_conv[j], conv_w_pw1[j], conv_b_pw1[j],
                                               conv_w_dw[j], conv_b_dw[j], conv_ln_g[j], conv_ln_b[j], conv_w_pw2[j],
                                               conv_b_pw2[j], g[3])
            conv_p.append(st_p); conv_s.append(st_s)
        xp, xs = xp3.reshape(b * t, d), xs3.reshape(s_n * n_tok, d)
        wgu, wdn = ffn_w_gu[i, 1].astype(BF16), ffn_w_down[i, 1].astype(BF16)
        xp, _ = _ffn(xp, g[4], g[5], g[5], wgu, wdn, False)
        xs, _ = _ffn(xs, g[4], g[5], g[5], wgu, wdn, False)
    return (xp.reshape(b, t, d), xs.reshape(s_n, n_tok, d), jnp.stack(cmp_p), jnp.stack(sel_p), jnp.stack(win_p),
            jnp.stack(conv_p), jnp.stack(cmp_s), jnp.stack(sel_s), jnp.stack(win_s), jnp.stack(conv_s))
```

```python
import functools

import numpy as np
import jax
import jax.numpy as jnp
from jax import lax
from jax.experimental import pallas as pl
from jax.experimental.pallas import tpu as pltpu

F32 = jnp.float32
BF16 = jnp.bfloat16

D_MODEL = 1024
N_HEADS = 16
HEAD_DIM = 64
N_KV = 4
GROUP = N_HEADS // N_KV
KV_W = N_KV * HEAD_DIM
Q_W = N_HEADS * HEAD_DIM
N_GATE = 3 * N_HEADS
CMP_LEN = 32
CMP_STRIDE = 16
SEL_BLOCK = 64
SEL_TOP = 16
WINDOW = 512
ROPE_THETA = 10000.0
CONV_WIDTH = 31
D_FF = 2816
EPS = 1e-6
NEG = -1e30
FORCED_KEY = 0x7F000000
LOG2E = 1.4426950408889634
Q_SCALE = HEAD_DIM ** -0.5 * LOG2E
PAGE_SIZE = 128

Q_BLOCK = 128
KEY_TILE = 512
MAX_SEL_BLOCKS = 128
FFN_CHUNK = 256
TOKEN_TILE = 512
CONV_HALO = 32
CMP_PAGES_PER_STEP = 32
SEL_PAGES_PER_STEP = 32
VMEM_LIMIT_MB = 56


def _cparams(n_axes, vmem_mb=VMEM_LIMIT_MB):
    return pltpu.CompilerParams(dimension_semantics=("arbitrary",) * n_axes,
                                vmem_limit_bytes=vmem_mb * 1024 * 1024)


def _const_spec(shape):
    nd = len(shape)
    return pl.BlockSpec(shape, lambda *_: (0,) * nd, pipeline_mode=pl.Buffered(1))


def _dot(a, b):
    return jnp.dot(a, b, preferred_element_type=F32)


def _dot_nt(a, b):
    return lax.dot_general(a, b, (((1,), (1,)), ((), ())), preferred_element_type=F32)


def _rms(x, g):
    return x * lax.rsqrt(jnp.mean(x * x, axis=-1, keepdims=True) + EPS) * g


def _sigmoid(x):
    return 1.0 / (1.0 + jnp.exp(-x))


def _rope_lanes(x, cos, sin):
    out = []
    for c in range(x.shape[1] // 128):
        xc = x[:, c * 128:(c + 1) * 128]
        lane = lax.broadcasted_iota(jnp.int32, xc.shape, 1)
        first = (lane % HEAD_DIM) < (HEAD_DIM // 2)
        swapped = jnp.where(first, pltpu.roll(xc, 128 - HEAD_DIM // 2, 1), pltpu.roll(xc, HEAD_DIM // 2, 1))
        out.append(xc * cos + swapped * sin)
    return out[0] if len(out) == 1 else jnp.concatenate(out, axis=1)


def _masked_softmax2(s, mask):
    s = jnp.where(mask, s, NEG)
    m = jnp.max(s, axis=-1, keepdims=True)
    p = jnp.where(mask, jnp.exp2(s - m), 0.0)
    d = jnp.sum(p, axis=-1, keepdims=True)
    return p * (1.0 / jnp.where(d > 0, d, 1.0))


def _lane_place(x, shift):
    shift %= x.shape[1]
    return x if shift == 0 else pltpu.roll(x, shift, 1)


def _head_lane_mask(shape, slot):
    lane = lax.broadcasted_iota(jnp.int32, shape, 1)
    return (lane // HEAD_DIM) == slot


def _ffn_kernel(x_ref, gpre_ref, gpost_ref, gnext_ref, wgu_ref, wdn_ref, xo_ref, *maybe_h, emit_h):
    x = x_ref[...]
    hn = _rms(x, gpre_ref[...]).astype(BF16)
    acc = jnp.zeros(x.shape, F32)
    for c in range(D_FF // FFN_CHUNK):
        lo = c * FFN_CHUNK
        g = _dot(hn, wgu_ref[:, lo:lo + FFN_CHUNK])
        u = _dot(hn, wgu_ref[:, D_FF + lo:D_FF + lo + FFN_CHUNK])
        a = (g * _sigmoid(g) * u).astype(BF16)
        acc = acc + _dot(a, wdn_ref[lo:lo + FFN_CHUNK, :])
    y = x + 0.5 * _rms(acc, gpost_ref[...])
    xo_ref[...] = y
    if emit_h:
        maybe_h[0][...] = _rms(y, gnext_ref[...]).astype(BF16)


def _ffn(x, g_pre, g_post, g_next, wgu, wdn, emit_h):
    n = x.shape[0]
    tm = min(TOKEN_TILE, n)
    row = pl.BlockSpec((tm, D_MODEL), lambda i: (i, 0))
    out_shape = [jax.ShapeDtypeStruct((n, D_MODEL), F32)]
    out_specs = [row]
    if emit_h:
        out_shape.append(jax.ShapeDtypeStruct((n, D_MODEL), BF16))
        out_specs.append(row)
    res = pl.pallas_call(
        functools.partial(_ffn_kernel, emit_h=emit_h),
        grid=(n // tm,),
        in_specs=[row, _const_spec((1, D_MODEL)), _const_spec((1, D_MODEL)), _const_spec((1, D_MODEL)),
                  _const_spec((D_MODEL, 2 * D_FF)), _const_spec((D_FF, D_MODEL))],
        out_specs=out_specs,
        out_shape=out_shape,
        compiler_params=_cparams(1),
    )(x, g_pre[None], g_post[None], g_next[None], wgu, wdn)
    return res if emit_h else (res[0], None)


def _proj_kernel(h_ref, cos_ref, sin_ref, wq_ref, wkv_ref, wg_ref,
                 q_ref, cmp_ref, sel_ref, win_ref, gate_ref, *prompt_refs):
    h = h_ref[...]
    cos = cos_ref[...]
    sin = sin_ref[...]
    q = _rope_lanes(_dot(h, wq_ref[...]), cos, sin)
    q_ref[...] = (q * Q_SCALE).astype(BF16)
    kv = _dot(h, wkv_ref[...])
    cmp_ref[...] = kv[:, 0:2 * KV_W]
    ks = _rope_lanes(kv[:, 2 * KV_W:3 * KV_W], cos, sin)
    vs = kv[:, 3 * KV_W:4 * KV_W]
    kw = _rope_lanes(kv[:, 4 * KV_W:5 * KV_W], cos, sin)
    vw = kv[:, 5 * KV_W:6 * KV_W]
    sel_ref[:, 0:KV_W] = ks
    sel_ref[:, KV_W:2 * KV_W] = vs
    win_ref[:, 0:KV_W] = kw
    win_ref[:, KV_W:2 * KV_W] = vw
    gate_ref[...] = _sigmoid(_dot(h, wg_ref[...]))
    if prompt_refs:
        kst_ref, vs1_ref, kwt_ref, vw1_ref = prompt_refs
        kst_ref[0] = ks.T.astype(BF16)
        kwt_ref[0] = kw.T.astype(BF16)
        for v, v1_ref in ((vs, vs1_ref), (vw, vw1_ref)):
            for g in range(N_KV):
                pair = v[:, (g // 2) * 128:(g // 2 + 1) * 128]
                if g % 2:
                    pair = pltpu.roll(pair, HEAD_DIM, 1)
                lane = lax.broadcasted_iota(jnp.int32, pair.shape, 1)
                v1_ref[0, g] = jnp.where(lane < HEAD_DIM, pair, 1.0).astype(BF16)


def _proj(h, cos, sin, wq, wkv, wg, batch=None):
    n = h.shape[0]
    tm = min(TOKEN_TILE, n)
    n_pos_tiles = cos.shape[0] // tm
    row = lambda w: pl.BlockSpec((tm, w), lambda i: (i, 0))
    tab = pl.BlockSpec((tm, 128), lambda i: (i % n_pos_tiles, 0))
    out_shape = [jax.ShapeDtypeStruct((n, Q_W), BF16), jax.ShapeDtypeStruct((n, 2 * KV_W), F32),
                 jax.ShapeDtypeStruct((n, 2 * KV_W), F32), jax.ShapeDtypeStruct((n, 2 * KV_W), F32),
                 jax.ShapeDtypeStruct((n, N_GATE), F32)]
    out_specs = [row(Q_W), row(2 * KV_W), row(2 * KV_W), row(2 * KV_W), row(N_GATE)]
    if batch is not None:
        t = n // batch
        nt = t // tm
        kt_spec = pl.BlockSpec((1, KV_W, tm), lambda i: (i // nt, 0, i % nt))
        v1_spec = pl.BlockSpec((1, N_KV, tm, 2 * HEAD_DIM), lambda i: (i // nt, 0, i % nt, 0))
        out_shape += [jax.ShapeDtypeStruct((batch, KV_W, t), BF16),
                      jax.ShapeDtypeStruct((batch, N_KV, t, 2 * HEAD_DIM), BF16)] * 2
        out_specs += [kt_spec, v1_spec] * 2
    return pl.pallas_call(
        _proj_kernel,
        grid=(n // tm,),
        in_specs=[row(D_MODEL), tab, tab, _const_spec((D_MODEL, Q_W)), _const_spec((D_MODEL, 6 * KV_W)),
                  _const_spec((D_MODEL, N_GATE))],
        out_specs=out_specs,
        out_shape=out_shape,
        compiler_params=_cparams(1),
    )(h, cos, sin, wq, wkv, wg)


def _compress_kernel(pt_ref, *refs, npg, nch, rows_minor):
    del pt_ref
    pages = refs[:npg]
    w1_ref, pe_ref, b1_ref, w2_ref, b2_ref, cos_ref, sin_ref, kt_ref, v_ref, part_ref = refs[npg:npg + 10]
    j = pl.program_id(1)
    rows = npg * (PAGE_SIZE // CMP_STRIDE)
    row0 = pl.multiple_of(j * rows, rows)
    if rows_minor:
        rows_scr = refs[npg + 10]
        for pi, p in enumerate(pages):
            for kv in range(2):
                t = p[0, 0, kv].reshape(KV_W, PAGE_SIZE).T
                for h in range(KV_W // 128):
                    rows_scr[kv, h, pi * PAGE_SIZE:(pi + 1) * PAGE_SIZE, :] = t[:, h * 128:(h + 1) * 128]
    for kv in range(2):
        acc = [jnp.zeros((rows, KV_W), F32) for _ in range(CMP_LEN // CMP_STRIDE)]
        for i in range(CMP_STRIDE):
            if rows_minor:
                x = jnp.concatenate([rows_scr[kv, h, pl.ds(i, rows, stride=CMP_STRIDE), :]
                                     for h in range(KV_W // 128)], axis=1)
            else:
                lo = i * 2 * KV_W + kv * KV_W
                x = jnp.concatenate([p[0, 0, :, lo:lo + KV_W] for p in pages], axis=0)
            for r in range(CMP_LEN // CMP_STRIDE):
                xr = (x + pe_ref[kv, r, i:i + 1, :]).astype(BF16)
                acc[r] = acc[r] + _dot(xr, w1_ref[kv, r, i])
        for r in range(CMP_LEN // CMP_STRIDE):
            part_ref[kv, pl.ds(row0, rows), r * KV_W:(r + 1) * KV_W] = acc[r]

    @pl.when(j == pl.num_programs(1) - 1)
    def _():
        for kv in range(2):
            hid = b1_ref[kv:kv + 1, :] + part_ref[kv, :, 0:KV_W]
            hid = hid + pltpu.roll(part_ref[kv, :, KV_W:2 * KV_W], nch - 1, 0)
            act = jax.nn.gelu(hid, approximate=True).astype(BF16)
            out = _dot(act, w2_ref[kv]) + b2_ref[kv:kv + 1, :]
            rowi = lax.broadcasted_iota(jnp.int32, out.shape, 0)
            out = jnp.where(rowi < nch - 1, out, 0.0)
            if kv == 0:
                out = _rope_lanes(out, cos_ref[...], sin_ref[...])
                kt_ref[0] = out.T.astype(BF16)
            else:
                v_ref[0] = out.astype(BF16)


def _page_index(s, j, pt, *, k, npg, layer, n_trail):
    return (layer, pt[s, j * npg + k]) + (0,) * n_trail


def _chunk_view(pool):
    return pool.reshape(pool.shape[0], pool.shape[1], PAGE_SIZE // CMP_STRIDE, CMP_STRIDE * 2 * KV_W)


def _compress(pool, layer, page_table, cw, rows_minor):
    s_n, p_n = page_table.shape
    chunks_per_page = PAGE_SIZE // CMP_STRIDE
    nch = p_n * chunks_per_page
    npg = min(CMP_PAGES_PER_STEP, p_n)
    page_block = pool.shape[2:]
    page_specs = [pl.BlockSpec((1, 1) + page_block,
                               functools.partial(_page_index, k=k, npg=npg, layer=layer, n_trail=len(page_block)))
                  for k in range(npg)]
    consts = [cw["w1"], cw["pe"], cw["b1"], cw["w2"], cw["b2"], cw["cos"], cw["sin"]]
    scratch = [pltpu.VMEM((2, nch, 2 * KV_W), F32)]
    if rows_minor:
        scratch.append(pltpu.VMEM((2, KV_W // 128, npg * PAGE_SIZE, 128), F32))
    grid_spec = pltpu.PrefetchScalarGridSpec(
        num_scalar_prefetch=1,
        grid=(s_n, p_n // npg),
        in_specs=page_specs + [_const_spec(c.shape) for c in consts],
        out_specs=[pl.BlockSpec((1, KV_W, nch), lambda s, j, pt: (s, 0, 0)),
                   pl.BlockSpec((1, nch, KV_W), lambda s, j, pt: (s, 0, 0))],
        scratch_shapes=scratch,
    )
    return pl.pallas_call(
        functools.partial(_compress_kernel, npg=npg, nch=nch, rows_minor=rows_minor),
        grid_spec=grid_spec,
        out_shape=[jax.ShapeDtypeStruct((s_n, KV_W, nch), BF16), jax.ShapeDtypeStruct((s_n, nch, KV_W), BF16)],
        compiler_params=_cparams(2),
    )(page_table, *([pool] * npg), *consts)


def _rank_values(imp_t, valid, forced):
    bits = pltpu.bitcast(imp_t, jnp.int32)
    return jnp.where(forced, FORCED_KEY, jnp.where(valid, bits, -1))


def _rank_counts(v_scr, groups, n_rank_rows):
    shape = v_scr.shape[1:]
    n_idx = lax.broadcasted_iota(jnp.int32, shape, 0)

    def body(m, cnts):
        later = jnp.where(n_idx > m, 1, 0)
        out = []
        for g, cnt in zip(groups, cnts):
            row = v_scr[g, pl.ds(m, 1), :]
            out.append(cnt + jnp.where(row > v_scr[g] - later, 1, 0))
        return tuple(out)

    return lax.fori_loop(0, n_rank_rows, body, tuple(jnp.zeros(shape, jnp.int32) for _ in groups))


def _importance_t(p_sum, ovt_ref):
    hi = p_sum.astype(BF16)
    lo = (p_sum - hi.astype(F32)).astype(BF16)
    return _dot_nt(ovt_ref[...], hi) + _dot_nt(ovt_ref[...], lo)


SELNEG_LANES = MAX_SEL_BLOCKS
Q_LANE0 = SELNEG_LANES


def _key_operand(mask_rows, k_rows):
    n = k_rows.shape[1]
    top = jnp.zeros((SELNEG_LANES, n), BF16) if mask_rows is None else mask_rows
    return jnp.concatenate([top, k_rows, jnp.zeros((KV_W - SELNEG_LANES - HEAD_DIM, n), BF16)], axis=0)


def _ones_normalise(acc):
    lane = lax.broadcasted_iota(jnp.int32, acc.shape, 1)
    return jnp.where(lane < HEAD_DIM, acc * (1.0 / pltpu.roll(acc, HEAD_DIM, 1)), 0.0)


def _biased(s, bias):
    n = s.shape[1]
    return (s.reshape(GROUP, Q_BLOCK, n) + bias[None]).reshape(s.shape)


def _attn_prompt_kernel(q_ref, gate_ref, x_ref, kcct_ref, vcc_ref, kst_ref, vs1_ref, kwt_ref, vw1_ref,
                        eneg_ref, ovt_ref, wout_ref, g3_ref, o_ref, qm_scr, oc_scr, v_scr, acc_scr):
    qb = pl.program_id(1)
    q0 = qb * Q_BLOCK
    rows = GROUP * Q_BLOCK
    nch = kcct_ref.shape[2]
    heads = lambda g: slice(g * HEAD_DIM, (g + 1) * HEAD_DIM)
    tok_pos = lambda n: q0 + lax.broadcasted_iota(jnp.int32, (Q_BLOCK, n), 0)
    key_idx = lambda n: lax.broadcasted_iota(jnp.int32, (Q_BLOCK, n), 1)

    c_bias = jnp.where(key_idx(nch) * CMP_STRIDE + (CMP_LEN - 1) <= tok_pos(nch), 0.0, NEG)
    n_idx = lax.broadcasted_iota(jnp.int32, (MAX_SEL_BLOCKS, Q_BLOCK), 0)
    cur = (q0 + lax.broadcasted_iota(jnp.int32, (MAX_SEL_BLOCKS, Q_BLOCK), 1)) // SEL_BLOCK
    valid = n_idx <= cur
    forced = valid & ((n_idx == 0) | (n_idx >= cur - 1))
    for g in range(N_KV):
        qg = q_ref[0, :, g * KV_W:(g + 1) * KV_W].astype(F32)
        q_lanes = _head_lane_mask(qg.shape, Q_LANE0 // HEAD_DIM)
        qm = jnp.concatenate(
            [jnp.where(q_lanes, _lane_place(qg, Q_LANE0 - j * HEAD_DIM), 0.0) for j in range(GROUP)], axis=0
        ).astype(BF16)
        qm_scr[g] = qm
        s_c = _biased(_dot(qm, _key_operand(None, kcct_ref[0, heads(g), :])), c_bias)
        m_c = jnp.max(s_c, axis=-1, keepdims=True)
        p_c = jnp.exp2(s_c - m_c)
        d_c = jnp.sum(p_c, axis=-1, keepdims=True)
        p_c = p_c * jnp.where(m_c > 0.5 * NEG, 1.0 / d_c, 0.0)
        o_c = _dot(p_c.astype(BF16), vcc_ref[0])
        o_c = o_c[:, (g // 2) * 128:(g // 2 + 1) * 128]
        oc_scr[g] = pltpu.roll(o_c, HEAD_DIM, 1) if g % 2 else o_c
        p_sum = p_c[0:Q_BLOCK]
        for j in range(1, GROUP):
            p_sum = p_sum + p_c[j * Q_BLOCK:(j + 1) * Q_BLOCK]
        v_scr[g] = _rank_values(_importance_t(p_sum, ovt_ref), valid, forced)

    n_live = jnp.minimum((q0 + Q_BLOCK) // SEL_BLOCK, MAX_SEL_BLOCKS)
    for pair in ((0, 1), (2, 3)):
        cnts = _rank_counts(v_scr, pair, n_live)
        for g, cnt in zip(pair, cnts):
            not_picked = jnp.where(valid & (cnt < SEL_TOP), 0.0, 1.0).T.astype(BF16)
            for j in range(GROUP):
                qm_scr[g, j * Q_BLOCK:(j + 1) * Q_BLOCK, 0:SELNEG_LANES] = not_picked

    def sel_tile(kt, ms):
        k0 = pl.multiple_of(kt * KEY_TILE, KEY_TILE)
        blk_bias = eneg_ref[:, pl.ds(k0, KEY_TILE)]
        tri = jnp.where(k0 + key_idx(KEY_TILE) <= tok_pos(KEY_TILE), 0.0, NEG)
        out = []
        for g in range(N_KV):
            s = _dot(qm_scr[g], _key_operand(blk_bias, kst_ref[0, heads(g), pl.ds(k0, KEY_TILE)]))
            s = _biased(s, tri)
            m_new = jnp.maximum(ms[g], jnp.max(s, axis=-1, keepdims=True))
            alpha = jnp.exp2(ms[g] - m_new)
            p = jnp.exp2(s - m_new).astype(BF16)
            acc_scr[g] = alpha * acc_scr[g] + _dot(p, vs1_ref[0, g, pl.ds(k0, KEY_TILE), :])
            out.append(m_new)
        return tuple(out)

    acc_scr[...] = jnp.zeros(acc_scr.shape, F32)
    n_tiles = (q0 + Q_BLOCK + KEY_TILE - 1) // KEY_TILE
    lax.fori_loop(0, n_tiles, sel_tile, tuple(jnp.full((rows, 1), NEG, F32) for _ in range(N_KV)))

    wlen = WINDOW + Q_BLOCK
    w0 = pl.multiple_of(jnp.maximum(q0 - WINDOW, 0), Q_BLOCK)
    w_key = w0 + key_idx(wlen)
    w_bias = jnp.where((w_key <= tok_pos(wlen)) & (w_key > tok_pos(wlen) - WINDOW), 0.0, NEG)
    slabs = []
    for g in range(N_KV):
        s_w = _biased(_dot(qm_scr[g], _key_operand(None, kwt_ref[0, heads(g), pl.ds(w0, wlen)])), w_bias)
        p_w = jnp.exp2(s_w - jnp.max(s_w, axis=-1, keepdims=True)).astype(BF16)
        o_w = _ones_normalise(_dot(p_w, vw1_ref[0, g, pl.ds(w0, wlen), :]))
        o_s = _ones_normalise(acc_scr[g])
        o_c = oc_scr[g]
        halves = [jnp.zeros((Q_BLOCK, 2 * HEAD_DIM), F32) for _ in range(GROUP // 2)]
        for j in range(GROUP):
            c = (g * GROUP + j) * 3
            rs = slice(j * Q_BLOCK, (j + 1) * Q_BLOCK)
            mix = (gate_ref[0, :, c:c + 1] * o_c[rs] + gate_ref[0, :, c + 1:c + 2] * o_s[rs]
                   + gate_ref[0, :, c + 2:c + 3] * o_w[rs])
            mix = jnp.where(_head_lane_mask(mix.shape, j % 2), _lane_place(mix, (j % 2) * HEAD_DIM), 0.0)
            halves[j // 2] = halves[j // 2] + mix
        slabs += halves
    o = jnp.concatenate(slabs, axis=1).astype(BF16)
    o_ref[0] = x_ref[0] + _rms(_dot(o, wout_ref[...]), g3_ref[...])


def _attn_prompt(q, gates, x, kcct, vcc, kst, vs1, kwt, vw1, eneg, ovt, wout, g3):
    b, t, _ = q.shape
    nch = kcct.shape[2]
    rows = GROUP * Q_BLOCK
    per_q = lambda w: pl.BlockSpec((1, Q_BLOCK, w), lambda bi, qi: (bi, qi, 0))
    per_b = lambda s: pl.BlockSpec((1,) + s, lambda bi, qi: (bi,) + (0,) * len(s), pipeline_mode=pl.Buffered(1))
    return pl.pallas_call(
        _attn_prompt_kernel,
        grid=(b, t // Q_BLOCK),
        in_specs=[per_q(Q_W), per_q(N_GATE), per_q(D_MODEL),
                  per_b((KV_W, nch)), per_b((nch, KV_W)),
                  per_b((KV_W, t)), per_b((N_KV, t, 2 * HEAD_DIM)), per_b((KV_W, t)), per_b((N_KV, t, 2 * HEAD_DIM)),
                  _const_spec(eneg.shape), _const_spec(ovt.shape), _const_spec((Q_W, D_MODEL)),
                  _const_spec((1, D_MODEL))],
        out_specs=per_q(D_MODEL),
        out_shape=jax.ShapeDtypeStruct((b, t, D_MODEL), F32),
        scratch_shapes=[pltpu.VMEM((N_KV, rows, KV_W), BF16), pltpu.VMEM((N_KV, rows, 2 * HEAD_DIM), F32),
                        pltpu.VMEM((N_KV, MAX_SEL_BLOCKS, Q_BLOCK), jnp.int32),
                        pltpu.VMEM((N_KV, rows, 2 * HEAD_DIM), F32)],
        compiler_params=_cparams(2),
    )(q, gates, x, kcct, vcc, kst, vs1, kwt, vw1, eneg, ovt, wout, g3[None])


TOK_PAD = 8


def _attn_sample_kernel(pt_ref, q_ref, gate_ref, x_ref, kcct_ref, vcc_ref, selnew_ref, winnew_ref, winbuf_ref,
                        *refs, npg, n_tok, past):
    del pt_ref
    pages = refs[:npg]
    (e_ref, ovt_ref, wout_ref, g3_ref, o_ref,
     qexp_scr, selrow_scr, m_scr, l_scr, acc_scr, oc_scr, v_scr) = refs[npg:]
    k = pl.program_id(1)
    rows = N_HEADS * TOK_PAD
    n_cache_blocks = past // SEL_BLOCK

    def row_tok(shape):
        return lax.broadcasted_iota(jnp.int32, shape, 0) % TOK_PAD

    def pad_rows(a, n):
        return jnp.concatenate([a, jnp.zeros((n - a.shape[0], a.shape[1]), a.dtype)], axis=0)

    @pl.when(k == 0)
    def _():
        q8 = q_ref[0].astype(F32)
        pieces = []
        for g in range(N_KV):
            qg = q8[:, g * KV_W:(g + 1) * KV_W]
            g_lanes = _head_lane_mask(qg.shape, g)
            for j in range(GROUP):
                pieces.append(jnp.where(g_lanes, _lane_place(qg, (g - j) * HEAD_DIM), 0.0))
        qexp = jnp.concatenate(pieces, axis=0).astype(BF16)
        qexp_scr[...] = qexp

        s_c = _dot(qexp, kcct_ref[0])
        c_end = lax.broadcasted_iota(jnp.int32, s_c.shape, 1) * CMP_STRIDE + (CMP_LEN - 1)
        p_c = _masked_softmax2(s_c, c_end <= past + row_tok(s_c.shape))
        oc_scr[...] = _dot(p_c.astype(BF16), vcc_ref[0])

        sums = []
        for g in range(N_KV):
            acc = p_c[(g * GROUP) * TOK_PAD:(g * GROUP + 1) * TOK_PAD]
            for j in range(1, GROUP):
                acc = acc + p_c[(g * GROUP + j) * TOK_PAD:(g * GROUP + j + 1) * TOK_PAD]
            sums.append(acc)
        p_sum = pad_rows(jnp.concatenate(sums, axis=0), MAX_SEL_BLOCKS)
        imp_t = _importance_t(p_sum, ovt_ref)
        n_idx = lax.broadcasted_iota(jnp.int32, imp_t.shape, 0)
        cur = (past + lax.broadcasted_iota(jnp.int32, imp_t.shape, 1) % TOK_PAD) // SEL_BLOCK
        valid = n_idx <= cur
        forced = valid & ((n_idx == 0) | (n_idx >= cur - 1))
        v_scr[0] = _rank_values(imp_t, valid, forced)
        (cnt,) = _rank_counts(v_scr, (0,), n_cache_blocks)
        sel = jnp.where(valid & (cnt < SEL_TOP - 1), 1.0, 0.0).T
        sel_rows = []
        for g in range(N_KV):
            sel_rows += [sel[g * TOK_PAD:(g + 1) * TOK_PAD]] * GROUP
        selrow_scr[...] = jnp.concatenate(sel_rows, axis=0).astype(BF16)
        m_scr[...] = jnp.full(m_scr.shape, NEG, F32)
        l_scr[...] = jnp.zeros(l_scr.shape, F32)
        acc_scr[...] = jnp.zeros(acc_scr.shape, F32)

    def online_update(s, v_tiles):
        m_old = m_scr[...]
        m_new = jnp.maximum(m_old, jnp.max(s, axis=-1, keepdims=True))
        alpha = jnp.exp2(m_old - m_new)
        p = jnp.exp2(s - m_new)
        l_scr[...] = alpha * l_scr[...] + jnp.sum(p, axis=-1, keepdims=True)
        acc = alpha * acc_scr[...]
        w = s.shape[1] // len(v_tiles)
        for i, (v, transposed) in enumerate(v_tiles):
            pv = (_dot_nt if transposed else _dot)
            acc = acc + pv(p[:, i * w:(i + 1) * w].astype(BF16), v)
        acc_scr[...] = acc
        m_scr[...] = m_new

    page_t = lambda p, kv: p[0, 0, kv].reshape(KV_W, PAGE_SIZE).astype(BF16)
    qexp = qexp_scr[...]
    keys = npg * PAGE_SIZE
    k0 = pl.multiple_of(k * keys, keys)
    s = jnp.concatenate([_dot(qexp, page_t(p, 0)) for p in pages], axis=1)
    picked = _dot(selrow_scr[...], e_ref[:, pl.ds(k0, keys)])
    s = s + jnp.where(picked > 0.5, 0.0, NEG)
    online_update(s, [(page_t(p, 1), True) for p in pages])

    @pl.when(k == pl.num_programs(1) - 1)
    def _():
        tok = row_tok((rows, PAGE_SIZE))
        new_i = lax.broadcasted_iota(jnp.int32, (rows, PAGE_SIZE), 1)
        new_ok = (new_i < n_tok) & (new_i <= tok)

        sel_new = pad_rows(selnew_ref[0], PAGE_SIZE)
        s_n = _dot_nt(qexp, sel_new[:, 0:KV_W].astype(BF16))
        online_update(jnp.where(new_ok, s_n, NEG), [(sel_new[:, KV_W:2 * KV_W].astype(BF16), False)])
        o_s = acc_scr[...] * (1.0 / l_scr[...])

        nbuf = winbuf_ref.shape[-1]
        buf_t = lambda kv: winbuf_ref[0, 0, kv].reshape(KV_W, nbuf).astype(BF16)
        win_new = pad_rows(winnew_ref[0], PAGE_SIZE)
        s_b = _dot(qexp, buf_t(0))
        s_nw = _dot_nt(qexp, win_new[:, 0:KV_W].astype(BF16))
        s_w = jnp.concatenate([s_b, s_nw], axis=1)
        col = lax.broadcasted_iota(jnp.int32, s_w.shape, 1)
        tok_w = row_tok(s_w.shape)
        in_buf = (col < nbuf) & (past - nbuf + col > past + tok_w - WINDOW) & (past - nbuf + col >= 0)
        in_new = (col >= nbuf) & (col - nbuf < n_tok) & (col - nbuf <= tok_w)
        p_w = _masked_softmax2(s_w, in_buf | in_new)
        o_w = (_dot_nt(p_w[:, 0:nbuf].astype(BF16), buf_t(1))
               + _dot(p_w[:, nbuf:].astype(BF16), win_new[:, KV_W:2 * KV_W].astype(BF16)))

        o_c = oc_scr[...]
        gates = gate_ref[0]
        slabs = []
        for g in range(N_KV):
            slab = jnp.zeros((TOK_PAD, KV_W), F32)
            for j in range(GROUP):
                c = (g * GROUP + j) * 3
                rs = slice((g * GROUP + j) * TOK_PAD, (g * GROUP + j + 1) * TOK_PAD)
                mix = gates[:, c:c + 1] * o_c[rs] + gates[:, c + 1:c + 2] * o_s[rs] + gates[:, c + 2:c + 3] * o_w[rs]
                slab = slab + jnp.where(_head_lane_mask(mix.shape, j), _lane_place(mix, (j - g) * HEAD_DIM), 0.0)
            slabs.append(slab)
        o = jnp.concatenate(slabs, axis=1).astype(BF16)
        y = _rms(_dot(o, wout_ref[...]), g3_ref[...])
        o_ref[0] = x_ref[0] + y


def _attn_sample(q, gates, x, kcct, vcc, sel_new, win_new, win_buf_t, sel_pool_t, layer, page_table, e_mat, ovt,
                 wout, g3, past):
    s_n, n_tok, _ = q.shape
    pad_tok = lambda a: jnp.pad(a, ((0, 0), (0, TOK_PAD - n_tok), (0, 0)))
    q, gates, x, sel_new, win_new = (pad_tok(a) for a in (q, gates, x, sel_new, win_new))
    p_n = page_table.shape[1]
    npg = min(SEL_PAGES_PER_STEP, p_n)
    nch = kcct.shape[2]
    nbuf = win_buf_t.shape[-1]
    rows = N_HEADS * TOK_PAD
    per_s = lambda shp: pl.BlockSpec((1,) + shp, lambda s, k, pt: (s, 0, 0))
    page_specs = [pl.BlockSpec((1, 1, 2, N_KV, HEAD_DIM, PAGE_SIZE),
                               functools.partial(_page_index, k=i, npg=npg, layer=layer, n_trail=4))
                  for i in range(npg)]
    buf_spec = pl.BlockSpec((1, 1, 2, N_KV, HEAD_DIM, nbuf), lambda s, k, pt: (layer, s, 0, 0, 0, 0))
    grid_spec = pltpu.PrefetchScalarGridSpec(
        num_scalar_prefetch=1,
        grid=(s_n, p_n // npg),
        in_specs=[per_s((TOK_PAD, Q_W)), per_s((TOK_PAD, N_GATE)), per_s((TOK_PAD, D_MODEL)),
                  per_s((KV_W, nch)), per_s((nch, KV_W)), per_s((TOK_PAD, 2 * KV_W)), per_s((TOK_PAD, 2 * KV_W)),
                  buf_spec] + page_specs +
                 [_const_spec(e_mat.shape), _const_spec(ovt.shape), _const_spec((Q_W, D_MODEL)),
                  _const_spec((1, D_MODEL))],
        out_specs=per_s((TOK_PAD, D_MODEL)),
        scratch_shapes=[pltpu.VMEM((rows, KV_W), BF16), pltpu.VMEM((rows, MAX_SEL_BLOCKS), BF16),
                        pltpu.VMEM((rows, 1), F32), pltpu.VMEM((rows, 1), F32), pltpu.VMEM((rows, KV_W), F32),
                        pltpu.VMEM((rows, KV_W), F32), pltpu.VMEM((1, MAX_SEL_BLOCKS, MAX_SEL_BLOCKS), jnp.int32)],
    )
    return pl.pallas_call(
        functools.partial(_attn_sample_kernel, npg=npg, n_tok=n_tok, past=past),
        grid_spec=grid_spec,
        out_shape=jax.ShapeDtypeStruct((s_n, TOK_PAD, D_MODEL), F32),
        compiler_params=_cparams(2),
    )(page_table, q, gates, x, kcct, vcc, sel_new, win_new, win_buf_t, *([sel_pool_t] * npg), e_mat, ovt, wout,
      g3[None])[:, :n_tok]


def _conv_tail(y, lng_ref, lnb_ref, w2_ref, b2_ref, g3_ref):
    mean = jnp.mean(y, axis=-1, keepdims=True)
    yc = y - mean
    yn = yc * lax.rsqrt(jnp.mean(yc * yc, axis=-1, keepdims=True) + EPS) * lng_ref[...] + lnb_ref[...]
    act = (yn * _sigmoid(yn)).astype(BF16)
    return _rms(_dot(act, w2_ref[...]) + b2_ref[...], g3_ref[...])


def _glu(h, w1_ref, b1_ref):
    ag = _dot(h, w1_ref[...]) + b1_ref[...]
    d = ag.shape[1] // 2
    return ag[:, 0:d] * _sigmoid(ag[:, d:])


def _conv_prompt_kernel(h_ref, x_ref, w1_ref, b1_ref, wdw_ref, bdw_ref, lng_ref, lnb_ref, w2_ref, b2_ref, g3_ref,
                        o_ref, st_ref, ctx_scr, shift_scr):
    ti = pl.program_id(1)
    tm = h_ref.shape[1]

    @pl.when(ti == 0)
    def _():
        ctx_scr[0:CONV_HALO, :] = jnp.zeros((CONV_HALO, ctx_scr.shape[1]), F32)

    ctx_scr[CONV_HALO:CONV_HALO + tm, :] = _glu(h_ref[0], w1_ref, b1_ref)
    off = CONV_HALO - (CONV_WIDTH - 1)
    span = shift_scr.shape[1]
    for r in range(1, 8):
        shift_scr[r - 1] = ctx_scr[r:r + span, :]
    y = jnp.zeros((tm, ctx_scr.shape[1]), F32)
    for kk in range(CONV_WIDTH):
        a, r = divmod(off + kk, 8)
        tap = ctx_scr[8 * a:8 * a + tm, :] if r == 0 else shift_scr[r - 1, 8 * a:8 * a + tm, :]
        y = y + tap * wdw_ref[kk:kk + 1, :]
    y = y + bdw_ref[...]
    o_ref[0] = x_ref[0] + _conv_tail(y, lng_ref, lnb_ref, w2_ref, b2_ref, g3_ref)

    @pl.when(ti == pl.num_programs(1) - 1)
    def _():
        st_ref[0] = ctx_scr[tm + off:tm + CONV_HALO, :]

    ctx_scr[0:CONV_HALO, :] = ctx_scr[tm:tm + CONV_HALO, :]


def _conv_prompt(h, x, cv, g3):
    b, t, _ = h.shape
    d_in = cv["wdw"].shape[1]
    tm = min(TOKEN_TILE, t)
    row = pl.BlockSpec((1, tm, D_MODEL), lambda bi, ti: (bi, ti, 0))
    consts = [cv["w1"], cv["b1"], cv["wdw"], cv["bdw"], cv["lng"], cv["lnb"], cv["w2"], cv["b2"], g3[None]]
    return pl.pallas_call(
        _conv_prompt_kernel,
        grid=(b, t // tm),
        in_specs=[row, row] + [_const_spec(c.shape) for c in consts],
        out_specs=[row, pl.BlockSpec((1, CONV_WIDTH - 1, d_in), lambda bi, ti: (bi, 0, 0))],
        out_shape=[jax.ShapeDtypeStruct((b, t, D_MODEL), F32),
                   jax.ShapeDtypeStruct((b, CONV_WIDTH - 1, d_in), F32)],
        scratch_shapes=[pltpu.VMEM((tm + CONV_HALO, d_in), F32), pltpu.VMEM((7, tm + CONV_HALO - 8, d_in), F32)],
        compiler_params=_cparams(2),
    )(h, x, *consts)


def _conv_sample_kernel(h_ref, x_ref, st_ref, w1_ref, b1_ref, wdw_ref, bdw_ref, lng_ref, lnb_ref, w2_ref, b2_ref,
                        g3_ref, o_ref, sto_ref, *, n_tok, n_seq):
    u = _glu(h_ref[...], w1_ref, b1_ref)
    n_state = CONV_WIDTH - 1

    def ctx(i):
        return st_ref[i] if i < n_state else u[(i - n_state) * n_seq:(i - n_state + 1) * n_seq]

    ys = []
    for t in range(n_tok):
        y = jnp.zeros((n_seq, u.shape[1]), F32)
        for kk in range(CONV_WIDTH):
            y = y + ctx(t + kk) * wdw_ref[kk:kk + 1, :]
        ys.append(y)
    y = jnp.concatenate(ys, axis=0) + bdw_ref[...]
    o_ref[...] = x_ref[...] + _conv_tail(y, lng_ref, lnb_ref, w2_ref, b2_ref, g3_ref)
    for i in range(n_state):
        sto_ref[i] = ctx(i + n_tok)


def _conv_sample(h_ts, x_ts, state_t, cv, g3, n_tok):
    n, _ = h_ts.shape
    n_seq = n // n_tok
    d_in = cv["wdw"].shape[1]
    args = [h_ts, x_ts, state_t, cv["w1"], cv["b1"], cv["wdw"], cv["bdw"], cv["lng"], cv["lnb"], cv["w2"],
            cv["b2"], g3[None]]
    return pl.pallas_call(
        functools.partial(_conv_sample_kernel, n_tok=n_tok, n_seq=n_seq),
        grid=(1,),
        in_specs=[_const_spec(a.shape) for a in args],
        out_specs=[pl.BlockSpec((n, D_MODEL), lambda i: (0, 0)),
                   pl.BlockSpec((CONV_WIDTH - 1, n_seq, d_in), lambda i: (0, 0, 0))],
        out_shape=[jax.ShapeDtypeStruct((n, D_MODEL), F32),
                   jax.ShapeDtypeStruct((CONV_WIDTH - 1, n_seq, d_in), F32)],
        compiler_params=_cparams(1),
    )(*args)


def _rope_tables(pos):
    half = HEAD_DIM // 2
    inv = ROPE_THETA ** (-jnp.arange(half, dtype=F32) / half)
    ang = pos.astype(F32)[:, None] * inv[None, :]
    cos, sin = jnp.cos(ang), jnp.sin(ang)
    return jnp.concatenate([cos, cos, cos, cos], axis=1), jnp.concatenate([-sin, sin, -sin, sin], axis=1)


def _block_expand_matrix(n_keys):
    n = np.arange(MAX_SEL_BLOCKS)[:, None]
    k = np.arange(n_keys)[None, :]
    return jnp.asarray((k // SEL_BLOCK == n).astype(np.float32), dtype=BF16)


def _overlap_t(nch):
    n = np.arange(MAX_SEL_BLOCKS)[:, None] * SEL_BLOCK
    c = np.arange(nch)[None, :] * CMP_STRIDE
    ov = (c < n + SEL_BLOCK) & (c + CMP_LEN > n) & (np.arange(nch)[None, :] < nch - 1)
    return jnp.asarray(ov.astype(np.float32), dtype=BF16)


def _compress_weights(pe, w1, b1, w2, b2, nch):
    ratio = CMP_LEN // CMP_STRIDE
    eye = jnp.eye(N_KV, dtype=F32)
    w1r = w1.reshape(2, ratio, CMP_STRIDE, HEAD_DIM, HEAD_DIM)
    w1bd = jnp.einsum("gh,kride->krigdhe", eye, w1r).reshape(2, ratio, CMP_STRIDE, KV_W, KV_W).astype(BF16)
    w2bd = jnp.einsum("gh,kde->kgdhe", eye, w2).reshape(2, KV_W, KV_W).astype(BF16)
    cos, sin = _rope_tables(jnp.arange(nch) * CMP_STRIDE + CMP_LEN - 1)
    return {
        "w1": w1bd,
        "pe": jnp.tile(pe.reshape(2, ratio, CMP_STRIDE, HEAD_DIM), (1, 1, 1, N_KV)),
        "b1": jnp.tile(b1, (1, N_KV)),
        "w2": w2bd,
        "b2": jnp.tile(b2, (1, N_KV)),
        "cos": cos,
        "sin": sin,
    }


def _block_bias_matrix(n_keys):
    n = np.arange(MAX_SEL_BLOCKS)[:, None]
    k = np.arange(n_keys)[None, :]
    return jnp.asarray(np.where(k // SEL_BLOCK == n, NEG, 0.0).astype(np.float32), dtype=BF16)


def _rows_minor(a):
    nd = a.ndim
    return a.transpose(tuple(range(nd - 4)) + (nd - 3, nd - 2, nd - 1, nd - 4))


def _nsa_layer(hp, xp, hs, xs, cmp_pool_t, sel_pool_t, win_buf_t, win_buf, layer, page_table, w_in, w_out, pe, w1, b1,
               w2, b2, g3):
    b, t, _ = xp.shape
    s_n, n_tok, _ = xs.shape
    past = page_table.shape[1] * PAGE_SIZE
    assert t % TOKEN_TILE == 0 and t // SEL_BLOCK <= MAX_SEL_BLOCKS and t >= WINDOW + Q_BLOCK
    assert past // SEL_BLOCK == MAX_SEL_BLOCKS and n_tok <= TOK_PAD and past % SEL_BLOCK == 0
    wq = w_in[:, :Q_W].astype(BF16)
    wkv = w_in[:, Q_W:Q_W + 6 * KV_W].astype(BF16)
    wg = w_in[:, Q_W + 6 * KV_W:].astype(BF16)
    wout = w_out.astype(BF16)
    nch_p = t // CMP_STRIDE
    cw = _compress_weights(pe, w1, b1, w2, b2, nch_p)

    cos_p, sin_p = _rope_tables(jnp.arange(t))
    q, cmp_kv, sel_kv, win_kv, gates, kst, vsb, kwt, vwb = _proj(hp.reshape(b * t, D_MODEL), cos_p, sin_p, wq, wkv,
                                                                  wg, batch=b)
    ident_pages = jnp.arange(b * t // PAGE_SIZE, dtype=jnp.int32).reshape(b, t // PAGE_SIZE)
    kcct, vcc = _compress(_chunk_view(cmp_kv.reshape(1, b * t // PAGE_SIZE, PAGE_SIZE, 2 * KV_W)), 0, ident_pages, cw,
                          rows_minor=False)
    xp_new = _attn_prompt(q.reshape(b, t, Q_W), gates.reshape(b, t, N_GATE), xp, kcct, vcc, kst, vsb, kwt, vwb,
                          _block_bias_matrix(t), _overlap_t(nch_p), wout, g3)
    kv6 = lambda a, lead: a.reshape(lead + (2, N_KV, HEAD_DIM))
    n_win = min(WINDOW, t)
    outs_p = (kv6(cmp_kv, (b, t)), kv6(sel_kv, (b, t)), kv6(win_kv, (b, t))[:, t - n_win:])

    nch_s = past // CMP_STRIDE
    cw_s = cw if nch_s == nch_p else _compress_weights(pe, w1, b1, w2, b2, nch_s)
    cos_s, sin_s = _rope_tables(past + jnp.arange(s_n * n_tok) % n_tok)
    q_s, cmp_s, sel_s, win_s, gates_s = _proj(hs.reshape(s_n * n_tok, D_MODEL), cos_s, sin_s, wq, wkv, wg)
    kcct_s, vcc_s = _compress(cmp_pool_t, layer, page_table, cw_s, rows_minor=True)
    per_seq = lambda a: a.reshape(s_n, n_tok, a.shape[-1])
    nbuf = win_buf.shape[1]
    xs_new = _attn_sample(per_seq(q_s), per_seq(gates_s), xs, kcct_s, vcc_s, per_seq(sel_s), per_seq(win_s),
                          win_buf_t, sel_pool_t, layer, page_table, _block_expand_matrix(past), _overlap_t(nch_s),
                          wout, g3, past)
    win_all = jnp.concatenate([win_buf, kv6(win_s, (s_n, n_tok))], axis=1)[:, n_tok:]
    outs_s = (kv6(cmp_s, (s_n, n_tok)), kv6(sel_s, (s_n, n_tok)), win_all)
    return xp_new, xs_new, outs_p, outs_s


def _conv_layer(hp, xp, hs, xs, state, w_pw1, b_pw1, w_dw, b_dw, ln_g, ln_b, w_pw2, b_pw2, g3):
    s_n, n_tok, _ = xs.shape
    cv = {"w1": w_pw1.astype(BF16), "b1": b_pw1[None], "wdw": w_dw, "bdw": b_dw[None], "lng": ln_g[None],
          "lnb": ln_b[None], "w2": w_pw2.astype(BF16), "b2": b_pw2[None]}
    xp_new, st_p = _conv_prompt(hp, xp, cv, g3)
    to_ts = lambda a: a.reshape(s_n, n_tok, -1).transpose(1, 0, 2).reshape(s_n * n_tok, -1)
    xs_ts, st_t = _conv_sample(to_ts(hs), to_ts(xs), state.transpose(1, 0, 2), cv, g3, n_tok)
    xs_new = xs_ts.reshape(n_tok, s_n, -1).transpose(1, 0, 2)
    return xp_new, xs_new, st_p, st_t.transpose(1, 0, 2)


def kernel(x_prompt, x_sample, cache_cmp_kv, cache_sel_kv, state_win_kv, state_conv, page_table, norm_g, ffn_w_gu,
           ffn_w_down, attn_w_in, attn_w_out, cmp_pe, cmp_w1, cmp_b1, cmp_w2, cmp_b2, conv_w_pw1, conv_b_pw1,
           conv_w_dw, conv_b_dw, conv_ln_g, conv_ln_b, conv_w_pw2, conv_b_pw2):
    b, t, d = x_prompt.shape
    s_n, n_tok, _ = x_sample.shape
    depth = norm_g.shape[0]
    xp = x_prompt.reshape(b * t, d)
    xs = x_sample.reshape(s_n * n_tok, d)
    cmp_p, sel_p, win_p, conv_p = [], [], [], []
    cmp_s, sel_s, win_s, conv_s = [], [], [], []
    cmp_pool_t = _rows_minor(cache_cmp_kv)
    sel_pool_t = _rows_minor(cache_sel_kv)
    win_buf_t = _rows_minor(state_win_kv)
    for i in range(depth):
        g = norm_g[i]
        wgu, wdn = ffn_w_gu[i, 0].astype(BF16), ffn_w_down[i, 0].astype(BF16)
        xp, hp = _ffn(xp, g[0], g[1], g[2], wgu, wdn, True)
        xs, hs = _ffn(xs, g[0], g[1], g[2], wgu, wdn, True)
        j = i // 2
        xp3, xs3 = xp.reshape(b, t, d), xs.reshape(s_n, n_tok, d)
        hp3, hs3 = hp.reshape(b, t, d), hs.reshape(s_n, n_tok, d)
        if i % 2 == 0:
            xp3, xs3, o_p, o_s = _nsa_layer(hp3, xp3, hs3, xs3, cmp_pool_t, sel_pool_t, win_buf_t, state_win_kv[j], j,
                                            page_table, attn_w_in[j], attn_w_out[j], cmp_pe[j], cmp_w1[j], cmp_b1[j],
                                            cmp_w2[j], cmp_b2[j], g[3])
            cmp_p.append(o_p[0]); sel_p.append(o_p[1]); win_p.append(o_p[2])
            cmp_s.append(o_s[0]); sel_s.append(o_s[1]); win_s.append(o_s[2])
        else:
            xp3, xs3, st_p, st_s = _conv_layer(hp3, xp3, hs3, xs3, state_conv[j], conv_w_pw1[j], conv_b_pw1[j],
                                               conv_w_dw[j], conv_b_dw[j], conv_ln_g[j], conv_ln_b[j], conv_w_pw2[j],
                                               conv_b_pw2[j], g[3])
            conv_p.append(st_p); conv_s.append(st_s)
        xp, xs = xp3.reshape(b * t, d), xs3.reshape(s_n * n_tok, d)
        wgu, wdn = ffn_w_gu[i, 1].astype(BF16), ffn_w_down[i, 1].astype(BF16)
        xp, _ = _ffn(xp, g[4], g[5], g[5], wgu, wdn, False)
        xs, _ = _ffn(xs, g[4], g[5], g[5], wgu, wdn, False)
    return (xp.reshape(b, t, d), xs.reshape(s_n, n_tok, d), jnp.stack(cmp_p), jnp.stack(sel_p), jnp.stack(win_p),
            jnp.stack(conv_p), jnp.stack(cmp_s), jnp.stack(sel_s), jnp.stack(win_s), jnp.stack(conv_s))
```

```python
import functools

import numpy as np
import jax
import jax.numpy as jnp
from jax import lax
from jax.experimental import pallas as pl
from jax.experimental.pallas import tpu as pltpu

F32 = jnp.float32
BF16 = jnp.bfloat16

D_MODEL = 1024
N_HEADS = 16
HEAD_DIM = 64
N_KV = 4
GROUP = N_HEADS // N_KV
KV_W = N_KV * HEAD_DIM
Q_W = N_HEADS * HEAD_DIM
N_GATE = 3 * N_HEADS
CMP_LEN = 32
CMP_STRIDE = 16
SEL_BLOCK = 64
SEL_TOP = 16
WINDOW = 512
ROPE_THETA = 10000.0
CONV_WIDTH = 31
D_FF = 2816
EPS = 1e-6
NEG = -1e30
FORCED_KEY = 0x7F000000
LOG2E = 1.4426950408889634
Q_SCALE = HEAD_DIM ** -0.5 * LOG2E
PAGE_SIZE = 128

Q_BLOCK = 128
RANK_GROUPS_PER_LOOP = max(1, 256 // Q_BLOCK)
KEY_TILE = 512
MAX_SEL_BLOCKS = 128
FFN_CHUNK = 256
TOKEN_TILE = 512
CONV_HALO = 32
CMP_PAGES_PER_STEP = 32
SEL_PAGES_PER_STEP = 32
VMEM_LIMIT_MB = 56


def _cparams(n_axes, vmem_mb=VMEM_LIMIT_MB):
    return pltpu.CompilerParams(dimension_semantics=("arbitrary",) * n_axes,
                                vmem_limit_bytes=vmem_mb * 1024 * 1024)


def _const_spec(shape):
    nd = len(shape)
    return pl.BlockSpec(shape, lambda *_: (0,) * nd, pipeline_mode=pl.Buffered(1))


def _dot(a, b):
    return jnp.dot(a, b, preferred_element_type=F32)


def _dot_nt(a, b):
    return lax.dot_general(a, b, (((1,), (1,)), ((), ())), preferred_element_type=F32)


def _rms(x, g):
    return x * lax.rsqrt(jnp.mean(x * x, axis=-1, keepdims=True) + EPS) * g


def _sigmoid(x):
    return 1.0 / (1.0 + jnp.exp(-x))


def _rope_lanes(x, cos, sin):
    out = []
    for c in range(x.shape[1] // 128):
        xc = x[:, c * 128:(c + 1) * 128]
        lane = lax.broadcasted_iota(jnp.int32, xc.shape, 1)
        first = (lane % HEAD_DIM) < (HEAD_DIM // 2)
        swapped = jnp.where(first, pltpu.roll(xc, 128 - HEAD_DIM // 2, 1), pltpu.roll(xc, HEAD_DIM // 2, 1))
        out.append(xc * cos + swapped * sin)
    return out[0] if len(out) == 1 else jnp.concatenate(out, axis=1)


def _masked_softmax2(s, mask):
    s = jnp.where(mask, s, NEG)
    m = jnp.max(s, axis=-1, keepdims=True)
    p = jnp.where(mask, jnp.exp2(s - m), 0.0)
    d = jnp.sum(p, axis=-1, keepdims=True)
    return p * (1.0 / jnp.where(d > 0, d, 1.0))


def _lane_place(x, shift):
    shift %= x.shape[1]
    return x if shift == 0 else pltpu.roll(x, shift, 1)


def _head_lane_mask(shape, slot):
    lane = lax.broadcasted_iota(jnp.int32, shape, 1)
    return (lane // HEAD_DIM) == slot


def _ffn_kernel(x_ref, gpre_ref, gpost_ref, gnext_ref, wgu_ref, wdn_ref, xo_ref, *maybe_h, emit_h):
    x = x_ref[...]
    hn = _rms(x, gpre_ref[...]).astype(BF16)
    acc = jnp.zeros(x.shape, F32)
    for c in range(D_FF // FFN_CHUNK):
        lo = c * FFN_CHUNK
        g = _dot(hn, wgu_ref[:, lo:lo + FFN_CHUNK])
        u = _dot(hn, wgu_ref[:, D_FF + lo:D_FF + lo + FFN_CHUNK])
        a = (g * _sigmoid(g) * u).astype(BF16)
        acc = acc + _dot(a, wdn_ref[lo:lo + FFN_CHUNK, :])
    y = x + 0.5 * _rms(acc, gpost_ref[...])
    xo_ref[...] = y
    if emit_h:
        maybe_h[0][...] = _rms(y, gnext_ref[...]).astype(BF16)


def _ffn(x, g_pre, g_post, g_next, wgu, wdn, emit_h):
    n = x.shape[0]
    tm = min(TOKEN_TILE, n)
    row = pl.BlockSpec((tm, D_MODEL), lambda i: (i, 0))
    out_shape = [jax.ShapeDtypeStruct((n, D_MODEL), F32)]
    out_specs = [row]
    if emit_h:
        out_shape.append(jax.ShapeDtypeStruct((n, D_MODEL), BF16))
        out_specs.append(row)
    res = pl.pallas_call(
        functools.partial(_ffn_kernel, emit_h=emit_h),
        grid=(n // tm,),
        in_specs=[row, _const_spec((1, D_MODEL)), _const_spec((1, D_MODEL)), _const_spec((1, D_MODEL)),
                  _const_spec((D_MODEL, 2 * D_FF)), _const_spec((D_FF, D_MODEL))],
        out_specs=out_specs,
        out_shape=out_shape,
        compiler_params=_cparams(1),
    )(x, g_pre[None], g_post[None], g_next[None], wgu, wdn)
    return res if emit_h else (res[0], None)


def _proj_kernel(h_ref, cos_ref, sin_ref, wq_ref, wkv_ref, wg_ref,
                 q_ref, cmp_ref, sel_ref, win_ref, gate_ref, *prompt_refs):
    h = h_ref[...]
    cos = cos_ref[...]
    sin = sin_ref[...]
    q = _rope_lanes(_dot(h, wq_ref[...]), cos, sin)
    q_ref[...] = (q * Q_SCALE).astype(BF16)
    kv = _dot(h, wkv_ref[...])
    cmp_ref[...] = kv[:, 0:2 * KV_W]
    ks = _rope_lanes(kv[:, 2 * KV_W:3 * KV_W], cos, sin)
    vs = kv[:, 3 * KV_W:4 * KV_W]
    kw = _rope_lanes(kv[:, 4 * KV_W:5 * KV_W], cos, sin)
    vw = kv[:, 5 * KV_W:6 * KV_W]
    sel_ref[:, 0:KV_W] = ks
    sel_ref[:, KV_W:2 * KV_W] = vs
    win_ref[:, 0:KV_W] = kw
    win_ref[:, KV_W:2 * KV_W] = vw
    gate_ref[...] = _sigmoid(_dot(h, wg_ref[...]))
    if prompt_refs:
        kst_ref, vs1_ref, kwt_ref, vw1_ref = prompt_refs
        kst_ref[0] = ks.T.astype(BF16)
        kwt_ref[0] = kw.T.astype(BF16)
        for v, v1_ref in ((vs, vs1_ref), (vw, vw1_ref)):
            for g in range(N_KV):
                pair = v[:, (g // 2) * 128:(g // 2 + 1) * 128]
                if g % 2:
                    pair = pltpu.roll(pair, HEAD_DIM, 1)
                lane = lax.broadcasted_iota(jnp.int32, pair.shape, 1)
                v1_ref[0, g] = jnp.where(lane < HEAD_DIM, pair, 1.0).astype(BF16)


def _proj(h, cos, sin, wq, wkv, wg, batch=None):
    n = h.shape[0]
    tm = min(TOKEN_TILE, n)
    n_pos_tiles = cos.shape[0] // tm
    row = lambda w: pl.BlockSpec((tm, w), lambda i: (i, 0))
    tab = pl.BlockSpec((tm, 128), lambda i: (i % n_pos_tiles, 0))
    out_shape = [jax.ShapeDtypeStruct((n, Q_W), BF16), jax.ShapeDtypeStruct((n, 2 * KV_W), F32),
                 jax.ShapeDtypeStruct((n, 2 * KV_W), F32), jax.ShapeDtypeStruct((n, 2 * KV_W), F32),
                 jax.ShapeDtypeStruct((n, N_GATE), F32)]
    out_specs = [row(Q_W), row(2 * KV_W), row(2 * KV_W), row(2 * KV_W), row(N_GATE)]
    if batch is not None:
        t = n // batch
        nt = t // tm
        kt_spec = pl.BlockSpec((1, KV_W, tm), lambda i: (i // nt, 0, i % nt))
        v1_spec = pl.BlockSpec((1, N_KV, tm, 2 * HEAD_DIM), lambda i: (i // nt, 0, i % nt, 0))
        out_shape += [jax.ShapeDtypeStruct((batch, KV_W, t), BF16),
                      jax.ShapeDtypeStruct((batch, N_KV, t, 2 * HEAD_DIM), BF16)] * 2
        out_specs += [kt_spec, v1_spec] * 2
    return pl.pallas_call(
        _proj_kernel,
        grid=(n // tm,),
        in_specs=[row(D_MODEL), tab, tab, _const_spec((D_MODEL, Q_W)), _const_spec((D_MODEL, 6 * KV_W)),
                  _const_spec((D_MODEL, N_GATE))],
        out_specs=out_specs,
        out_shape=out_shape,
        compiler_params=_cparams(1),
    )(h, cos, sin, wq, wkv, wg)


def _compress_kernel(pt_ref, *refs, npg, nch, rows_minor):
    del pt_ref
    pages = refs[:npg]
    w1_ref, pe_ref, b1_ref, w2_ref, b2_ref, cos_ref, sin_ref, kt_ref, v_ref, part_ref = refs[npg:npg + 10]
    j = pl.program_id(1)
    rows = npg * (PAGE_SIZE // CMP_STRIDE)
    row0 = pl.multiple_of(j * rows, rows)
    if rows_minor:
        rows_scr = refs[npg + 10]
        for pi, p in enumerate(pages):
            for kv in range(2):
                t = p[0, 0, kv].reshape(KV_W, PAGE_SIZE).T
                for h in range(KV_W // 128):
                    rows_scr[kv, h, pi * PAGE_SIZE:(pi + 1) * PAGE_SIZE, :] = t[:, h * 128:(h + 1) * 128]
    for kv in range(2):
        acc = [jnp.zeros((rows, KV_W), F32) for _ in range(CMP_LEN // CMP_STRIDE)]
        for i in range(CMP_STRIDE):
            if rows_minor:
                x = jnp.concatenate([rows_scr[kv, h, pl.ds(i, rows, stride=CMP_STRIDE), :]
                                     for h in range(KV_W // 128)], axis=1)
            else:
                lo = i * 2 * KV_W + kv * KV_W
                x = jnp.concatenate([p[0, 0, :, lo:lo + KV_W] for p in pages], axis=0)
            for r in range(CMP_LEN // CMP_STRIDE):
                xr = (x + pe_ref[kv, r, i:i + 1, :]).astype(BF16)
                acc[r] = acc[r] + _dot(xr, w1_ref[kv, r, i])
        for r in range(CMP_LEN // CMP_STRIDE):
            part_ref[kv, pl.ds(row0, rows), r * KV_W:(r + 1) * KV_W] = acc[r]

    @pl.when(j == pl.num_programs(1) - 1)
    def _():
        for kv in range(2):
            hid = b1_ref[kv:kv + 1, :] + part_ref[kv, :, 0:KV_W]
            hid = hid + pltpu.roll(part_ref[kv, :, KV_W:2 * KV_W], nch - 1, 0)
            act = jax.nn.gelu(hid, approximate=True).astype(BF16)
            out = _dot(act, w2_ref[kv]) + b2_ref[kv:kv + 1, :]
            rowi = lax.broadcasted_iota(jnp.int32, out.shape, 0)
            out = jnp.where(rowi < nch - 1, out, 0.0)
            if kv == 0:
                out = _rope_lanes(out, cos_ref[...], sin_ref[...])
                kt_ref[0] = out.T.astype(BF16)
            else:
                v_ref[0] = out.astype(BF16)


def _page_index(s, j, pt, *, k, npg, layer, n_trail):
    return (layer, pt[s, j * npg + k]) + (0,) * n_trail


def _chunk_view(pool):
    return pool.reshape(pool.shape[0], pool.shape[1], PAGE_SIZE // CMP_STRIDE, CMP_STRIDE * 2 * KV_W)


def _compress(pool, layer, page_table, cw, rows_minor):
    s_n, p_n = page_table.shape
    chunks_per_page = PAGE_SIZE // CMP_STRIDE
    nch = p_n * chunks_per_page
    npg = min(CMP_PAGES_PER_STEP, p_n)
    page_block = pool.shape[2:]
    page_specs = [pl.BlockSpec((1, 1) + page_block,
                               functools.partial(_page_index, k=k, npg=npg, layer=layer, n_trail=len(page_block)))
                  for k in range(npg)]
    consts = [cw["w1"], cw["pe"], cw["b1"], cw["w2"], cw["b2"], cw["cos"], cw["sin"]]
    scratch = [pltpu.VMEM((2, nch, 2 * KV_W), F32)]
    if rows_minor:
        scratch.append(pltpu.VMEM((2, KV_W // 128, npg * PAGE_SIZE, 128), F32))
    grid_spec = pltpu.PrefetchScalarGridSpec(
        num_scalar_prefetch=1,
        grid=(s_n, p_n // npg),
        in_specs=page_specs + [_const_spec(c.shape) for c in consts],
        out_specs=[pl.BlockSpec((1, KV_W, nch), lambda s, j, pt: (s, 0, 0)),
                   pl.BlockSpec((1, nch, KV_W), lambda s, j, pt: (s, 0, 0))],
        scratch_shapes=scratch,
    )
    return pl.pallas_call(
        functools.partial(_compress_kernel, npg=npg, nch=nch, rows_minor=rows_minor),
        grid_spec=grid_spec,
        out_shape=[jax.ShapeDtypeStruct((s_n, KV_W, nch), BF16), jax.ShapeDtypeStruct((s_n, nch, KV_W), BF16)],
        compiler_params=_cparams(2),
    )(page_table, *([pool] * npg), *consts)


def _rank_values(imp_t, valid, forced):
    bits = pltpu.bitcast(imp_t, jnp.int32)
    return jnp.where(forced, FORCED_KEY, jnp.where(valid, bits, -1))


def _rank_counts(v_scr, cnt_scr, tie_scr, groups, n_rank_rows):
    n_blocks, lanes = v_scr.shape[2:]
    sub = lax.broadcasted_iota(jnp.int32, (8, lanes), 0)
    for g in groups:
        cnt_scr[g] = jnp.zeros((n_blocks, lanes), jnp.int32)
        tie_scr[g] = jnp.zeros((n_blocks, lanes), jnp.int32)

    def body(mb, carry):
        base = pl.multiple_of(mb * 8, 8)
        side = [jnp.where(r > mb, 1, 0) for r in range(n_blocks // 8)]
        for g in groups:
            own = v_scr[g, 0, pl.ds(base, 8), :]
            ahead = [jnp.zeros((8, lanes), jnp.int32) for _ in range(n_blocks // 8)]
            ties = jnp.zeros((8, lanes), jnp.int32)
            for mi in range(8):
                row = v_scr[g, 0, pl.ds(base + mi, 1), :]
                for r in range(n_blocks // 8):
                    ahead[r] = ahead[r] + jnp.where(row > v_scr[g, side[r], r * 8:(r + 1) * 8, :], 1, 0)
                ties = ties + jnp.where((row == own) & (sub > mi), 1, 0)
            cnt_scr[g] += jnp.concatenate(ahead, axis=0)
            tie_scr[g, pl.ds(base, 8), :] += ties
        return carry

    lax.fori_loop(0, (n_rank_rows + 7) // 8, body, 0)
    return tuple(cnt_scr[g] + tie_scr[g] for g in groups)


def _importance_t(p_sum, ovt_ref):
    hi = p_sum.astype(BF16)
    lo = (p_sum - hi.astype(F32)).astype(BF16)
    return _dot_nt(ovt_ref[...], hi) + _dot_nt(ovt_ref[...], lo)


SELNEG_LANES = MAX_SEL_BLOCKS
Q_LANE0 = SELNEG_LANES


def _key_operand(mask_rows, k_rows):
    n = k_rows.shape[1]
    top = jnp.zeros((SELNEG_LANES, n), BF16) if mask_rows is None else mask_rows
    return jnp.concatenate([top, k_rows, jnp.zeros((KV_W - SELNEG_LANES - HEAD_DIM, n), BF16)], axis=0)


def _ones_normalise(acc):
    lane = lax.broadcasted_iota(jnp.int32, acc.shape, 1)
    return jnp.where(lane < HEAD_DIM, acc * (1.0 / pltpu.roll(acc, HEAD_DIM, 1)), 0.0)


def _biased(s, bias):
    n = s.shape[1]
    return (s.reshape(GROUP, Q_BLOCK, n) + bias[None]).reshape(s.shape)


def _attn_prompt_kernel(q_ref, gate_ref, x_ref, kcct_ref, vcc_ref, kst_ref, vs1_ref, kwt_ref, vw1_ref,
                        eneg_ref, ovt_ref, wout_ref, g3_ref, o_ref, qm_scr, oc_scr, ow_scr, v_scr, cnt_scr, tie_scr, acc_scr):
    qb = pl.program_id(1)
    q0 = qb * Q_BLOCK
    rows = GROUP * Q_BLOCK
    nch = kcct_ref.shape[2]
    heads = lambda g: slice(g * HEAD_DIM, (g + 1) * HEAD_DIM)
    tok_pos = lambda n: q0 + lax.broadcasted_iota(jnp.int32, (Q_BLOCK, n), 0)
    key_idx = lambda n: lax.broadcasted_iota(jnp.int32, (Q_BLOCK, n), 1)

    c_bias = jnp.where(key_idx(nch) * CMP_STRIDE + (CMP_LEN - 1) <= tok_pos(nch), 0.0, NEG)
    wlen = WINDOW + Q_BLOCK
    w0 = pl.multiple_of(jnp.maximum(q0 - WINDOW, 0), Q_BLOCK)
    w_key = w0 + key_idx(wlen)
    w_bias = jnp.where((w_key <= tok_pos(wlen)) & (w_key > tok_pos(wlen) - WINDOW), 0.0, NEG)
    n_idx = lax.broadcasted_iota(jnp.int32, (MAX_SEL_BLOCKS, Q_BLOCK), 0)
    cur = (q0 + lax.broadcasted_iota(jnp.int32, (MAX_SEL_BLOCKS, Q_BLOCK), 1)) // SEL_BLOCK
    valid = n_idx <= cur
    forced = valid & ((n_idx == 0) | (n_idx >= cur - 1))
    for g in range(N_KV):
        qg = q_ref[0, :, g * KV_W:(g + 1) * KV_W].astype(F32)
        q_lanes = _head_lane_mask(qg.shape, Q_LANE0 // HEAD_DIM)
        qm = jnp.concatenate(
            [jnp.where(q_lanes, _lane_place(qg, Q_LANE0 - j * HEAD_DIM), 0.0) for j in range(GROUP)], axis=0
        ).astype(BF16)
        qm_scr[g] = qm
        s_c = _biased(_dot(qm, _key_operand(None, kcct_ref[0, heads(g), :])), c_bias)
        m_c = jnp.max(s_c, axis=-1, keepdims=True)
        p_c = jnp.exp2(s_c - m_c)
        d_c = jnp.sum(p_c, axis=-1, keepdims=True)
        p_c = p_c * jnp.where(m_c > 0.5 * NEG, 1.0 / d_c, 0.0)
        o_c = _dot(p_c.astype(BF16), vcc_ref[0])
        o_c = o_c[:, (g // 2) * 128:(g // 2 + 1) * 128]
        oc_scr[g] = pltpu.roll(o_c, HEAD_DIM, 1) if g % 2 else o_c
        p_sum = p_c[0:Q_BLOCK]
        for j in range(1, GROUP):
            p_sum = p_sum + p_c[j * Q_BLOCK:(j + 1) * Q_BLOCK]
        keys = _rank_values(_importance_t(p_sum, ovt_ref), valid, forced)
        v_scr[g, 0] = keys
        v_scr[g, 1] = keys - 1
        s_w = _biased(_dot(qm, _key_operand(None, kwt_ref[0, heads(g), pl.ds(w0, wlen)])), w_bias)
        p_w = jnp.exp2(s_w - jnp.max(s_w, axis=-1, keepdims=True)).astype(BF16)
        ow_scr[g] = _ones_normalise(_dot(p_w, vw1_ref[0, g, pl.ds(w0, wlen), :]))

    n_live = jnp.minimum((q0 + Q_BLOCK) // SEL_BLOCK, MAX_SEL_BLOCKS)
    for g0 in range(0, N_KV, RANK_GROUPS_PER_LOOP):
        batch = tuple(range(g0, g0 + RANK_GROUPS_PER_LOOP))
        cnts = _rank_counts(v_scr, cnt_scr, tie_scr, batch, n_live)
        for g, cnt in zip(batch, cnts):
            not_picked = jnp.where(valid & (cnt < SEL_TOP), 0.0, 1.0).T.astype(BF16)
            for j in range(GROUP):
                qm_scr[g, j * Q_BLOCK:(j + 1) * Q_BLOCK, 0:SELNEG_LANES] = not_picked

    def sel_tile(kt, ms):
        k0 = pl.multiple_of(kt * KEY_TILE, KEY_TILE)
        blk_bias = eneg_ref[:, pl.ds(k0, KEY_TILE)]
        tri = jnp.where(k0 + key_idx(KEY_TILE) <= tok_pos(KEY_TILE), 0.0, NEG)
        out = []
        for g in range(N_KV):
            s = _dot(qm_scr[g], _key_operand(blk_bias, kst_ref[0, heads(g), pl.ds(k0, KEY_TILE)]))
            s = _biased(s, tri)
            m_new = jnp.maximum(ms[g], jnp.max(s, axis=-1, keepdims=True))
            alpha = jnp.exp2(ms[g] - m_new)
            p = jnp.exp2(s - m_new).astype(BF16)
            acc_scr[g] = alpha * acc_scr[g] + _dot(p, vs1_ref[0, g, pl.ds(k0, KEY_TILE), :])
            out.append(m_new)
        return tuple(out)

    acc_scr[...] = jnp.zeros(acc_scr.shape, F32)
    n_tiles = (q0 + Q_BLOCK + KEY_TILE - 1) // KEY_TILE
    lax.fori_loop(0, n_tiles, sel_tile, tuple(jnp.full((rows, 1), NEG, F32) for _ in range(N_KV)))

    slabs = []
    for g in range(N_KV):
        o_w = ow_scr[g]
        o_s = _ones_normalise(acc_scr[g])
        o_c = oc_scr[g]
        halves = [jnp.zeros((Q_BLOCK, 2 * HEAD_DIM), F32) for _ in range(GROUP // 2)]
        for j in range(GROUP):
            c = (g * GROUP + j) * 3
            rs = slice(j * Q_BLOCK, (j + 1) * Q_BLOCK)
            mix = (gate_ref[0, :, c:c + 1] * o_c[rs] + gate_ref[0, :, c + 1:c + 2] * o_s[rs]
                   + gate_ref[0, :, c + 2:c + 3] * o_w[rs])
            mix = jnp.where(_head_lane_mask(mix.shape, j % 2), _lane_place(mix, (j % 2) * HEAD_DIM), 0.0)
            halves[j // 2] = halves[j // 2] + mix
        slabs += halves
    o = jnp.concatenate(slabs, axis=1).astype(BF16)
    o_ref[0] = x_ref[0] + _rms(_dot(o, wout_ref[...]), g3_ref[...])


def _attn_prompt(q, gates, x, kcct, vcc, kst, vs1, kwt, vw1, eneg, ovt, wout, g3):
    b, t, _ = q.shape
    nch = kcct.shape[2]
    rows = GROUP * Q_BLOCK
    per_q = lambda w: pl.BlockSpec((1, Q_BLOCK, w), lambda bi, qi: (bi, qi, 0))
    per_b = lambda s: pl.BlockSpec((1,) + s, lambda bi, qi: (bi,) + (0,) * len(s), pipeline_mode=pl.Buffered(1))
    return pl.pallas_call(
        _attn_prompt_kernel,
        grid=(b, t // Q_BLOCK),
        in_specs=[per_q(Q_W), per_q(N_GATE), per_q(D_MODEL),
                  per_b((KV_W, nch)), per_b((nch, KV_W)),
                  per_b((KV_W, t)), per_b((N_KV, t, 2 * HEAD_DIM)), per_b((KV_W, t)), per_b((N_KV, t, 2 * HEAD_DIM)),
                  _const_spec(eneg.shape), _const_spec(ovt.shape), _const_spec((Q_W, D_MODEL)),
                  _const_spec((1, D_MODEL))],
        out_specs=per_q(D_MODEL),
        out_shape=jax.ShapeDtypeStruct((b, t, D_MODEL), F32),
        scratch_shapes=[pltpu.VMEM((N_KV, rows, KV_W), BF16), pltpu.VMEM((N_KV, rows, 2 * HEAD_DIM), F32),
                        pltpu.VMEM((N_KV, rows, 2 * HEAD_DIM), F32),
                        pltpu.VMEM((N_KV, 2, MAX_SEL_BLOCKS, Q_BLOCK), jnp.int32),
                        pltpu.VMEM((N_KV, MAX_SEL_BLOCKS, Q_BLOCK), jnp.int32),
                        pltpu.VMEM((N_KV, MAX_SEL_BLOCKS, Q_BLOCK), jnp.int32),
                        pltpu.VMEM((N_KV, rows, 2 * HEAD_DIM), F32)],
        compiler_params=_cparams(2),
    )(q, gates, x, kcct, vcc, kst, vs1, kwt, vw1, eneg, ovt, wout, g3[None])


TOK_PAD = 8


def _attn_sample_kernel(pt_ref, q_ref, gate_ref, x_ref, kcct_ref, vcc_ref, selnew_ref, winnew_ref, winbuf_ref,
                        *refs, npg, n_tok, past):
    del pt_ref
    pages = refs[:npg]
    (e_ref, ovt_ref, wout_ref, g3_ref, o_ref,
     qexp_scr, selrow_scr, m_scr, l_scr, acc_scr, oc_scr, v_scr, cnt_scr, tie_scr) = refs[npg:]
    k = pl.program_id(1)
    rows = N_HEADS * TOK_PAD
    n_cache_blocks = past // SEL_BLOCK

    def row_tok(shape):
        return lax.broadcasted_iota(jnp.int32, shape, 0) % TOK_PAD

    def pad_rows(a, n):
        return jnp.concatenate([a, jnp.zeros((n - a.shape[0], a.shape[1]), a.dtype)], axis=0)

    @pl.when(k == 0)
    def _():
        q8 = q_ref[0].astype(F32)
        pieces = []
        for g in range(N_KV):
            qg = q8[:, g * KV_W:(g + 1) * KV_W]
            g_lanes = _head_lane_mask(qg.shape, g)
            for j in range(GROUP):
                pieces.append(jnp.where(g_lanes, _lane_place(qg, (g - j) * HEAD_DIM), 0.0))
        qexp = jnp.concatenate(pieces, axis=0).astype(BF16)
        qexp_scr[...] = qexp

        s_c = _dot(qexp, kcct_ref[0])
        c_end = lax.broadcasted_iota(jnp.int32, s_c.shape, 1) * CMP_STRIDE + (CMP_LEN - 1)
        p_c = _masked_softmax2(s_c, c_end <= past + row_tok(s_c.shape))
        oc_scr[...] = _dot(p_c.astype(BF16), vcc_ref[0])

        sums = []
        for g in range(N_KV):
            acc = p_c[(g * GROUP) * TOK_PAD:(g * GROUP + 1) * TOK_PAD]
            for j in range(1, GROUP):
                acc = acc + p_c[(g * GROUP + j) * TOK_PAD:(g * GROUP + j + 1) * TOK_PAD]
            sums.append(acc)
        p_sum = pad_rows(jnp.concatenate(sums, axis=0), MAX_SEL_BLOCKS)
        imp_t = _importance_t(p_sum, ovt_ref)
        n_idx = lax.broadcasted_iota(jnp.int32, imp_t.shape, 0)
        cur = (past + lax.broadcasted_iota(jnp.int32, imp_t.shape, 1) % TOK_PAD) // SEL_BLOCK
        valid = n_idx <= cur
        forced = valid & ((n_idx == 0) | (n_idx >= cur - 1))
        keys = _rank_values(imp_t, valid, forced)
        v_scr[0, 0] = keys
        v_scr[0, 1] = keys - 1
        (cnt,) = _rank_counts(v_scr, cnt_scr, tie_scr, (0,), n_cache_blocks)
        sel = jnp.where(valid & (cnt < SEL_TOP - 1), 1.0, 0.0).T
        sel_rows = []
        for g in range(N_KV):
            sel_rows += [sel[g * TOK_PAD:(g + 1) * TOK_PAD]] * GROUP
        selrow_scr[...] = jnp.concatenate(sel_rows, axis=0).astype(BF16)
        m_scr[...] = jnp.full(m_scr.shape, NEG, F32)
        l_scr[...] = jnp.zeros(l_scr.shape, F32)
        acc_scr[...] = jnp.zeros(acc_scr.shape, F32)

    def online_update(s, v_tiles):
        m_old = m_scr[...]
        m_new = jnp.maximum(m_old, jnp.max(s, axis=-1, keepdims=True))
        alpha = jnp.exp2(m_old - m_new)
        p = jnp.exp2(s - m_new)
        l_scr[...] = alpha * l_scr[...] + jnp.sum(p, axis=-1, keepdims=True)
        acc = alpha * acc_scr[...]
        w = s.shape[1] // len(v_tiles)
        for i, (v, transposed) in enumerate(v_tiles):
            pv = (_dot_nt if transposed else _dot)
            acc = acc + pv(p[:, i * w:(i + 1) * w].astype(BF16), v)
        acc_scr[...] = acc
        m_scr[...] = m_new

    page_t = lambda p, kv: p[0, 0, kv].reshape(KV_W, PAGE_SIZE).astype(BF16)
    qexp = qexp_scr[...]
    keys = npg * PAGE_SIZE
    k0 = pl.multiple_of(k * keys, keys)
    s = jnp.concatenate([_dot(qexp, page_t(p, 0)) for p in pages], axis=1)
    picked = _dot(selrow_scr[...], e_ref[:, pl.ds(k0, keys)])
    s = s + jnp.where(picked > 0.5, 0.0, NEG)
    online_update(s, [(page_t(p, 1), True) for p in pages])

    @pl.when(k == pl.num_programs(1) - 1)
    def _():
        tok = row_tok((rows, PAGE_SIZE))
        new_i = lax.broadcasted_iota(jnp.int32, (rows, PAGE_SIZE), 1)
        new_ok = (new_i < n_tok) & (new_i <= tok)

        sel_new = pad_rows(selnew_ref[0], PAGE_SIZE)
        s_n = _dot_nt(qexp, sel_new[:, 0:KV_W].astype(BF16))
        online_update(jnp.where(new_ok, s_n, NEG), [(sel_new[:, KV_W:2 * KV_W].astype(BF16), False)])
        o_s = acc_scr[...] * (1.0 / l_scr[...])

        nbuf = winbuf_ref.shape[-1]
        buf_t = lambda kv: winbuf_ref[0, 0, kv].reshape(KV_W, nbuf).astype(BF16)
        win_new = pad_rows(winnew_ref[0], PAGE_SIZE)
        s_b = _dot(qexp, buf_t(0))
        s_nw = _dot_nt(qexp, win_new[:, 0:KV_W].astype(BF16))
        s_w = jnp.concatenate([s_b, s_nw], axis=1)
        col = lax.broadcasted_iota(jnp.int32, s_w.shape, 1)
        tok_w = row_tok(s_w.shape)
        in_buf = (col < nbuf) & (past - nbuf + col > past + tok_w - WINDOW) & (past - nbuf + col >= 0)
        in_new = (col >= nbuf) & (col - nbuf < n_tok) & (col - nbuf <= tok_w)
        p_w = _masked_softmax2(s_w, in_buf | in_new)
        o_w = (_dot_nt(p_w[:, 0:nbuf].astype(BF16), buf_t(1))
               + _dot(p_w[:, nbuf:].astype(BF16), win_new[:, KV_W:2 * KV_W].astype(BF16)))

        o_c = oc_scr[...]
        gates = gate_ref[0]
        slabs = []
        for g in range(N_KV):
            slab = jnp.zeros((TOK_PAD, KV_W), F32)
            for j in range(GROUP):
                c = (g * GROUP + j) * 3
                rs = slice((g * GROUP + j) * TOK_PAD, (g * GROUP + j + 1) * TOK_PAD)
                mix = gates[:, c:c + 1] * o_c[rs] + gates[:, c + 1:c + 2] * o_s[rs] + gates[:, c + 2:c + 3] * o_w[rs]
                slab = slab + jnp.where(_head_lane_mask(mix.shape, j), _lane_place(mix, (j - g) * HEAD_DIM), 0.0)
            slabs.append(slab)
        o = jnp.concatenate(slabs, axis=1).astype(BF16)
        y = _rms(_dot(o, wout_ref[...]), g3_ref[...])
        o_ref[0] = x_ref[0] + y


def _attn_sample(q, gates, x, kcct, vcc, sel_new, win_new, win_buf_t, sel_pool_t, layer, page_table, e_mat, ovt,
                 wout, g3, past):
    s_n, n_tok, _ = q.shape
    pad_tok = lambda a: jnp.pad(a, ((0, 0), (0, TOK_PAD - n_tok), (0, 0)))
    q, gates, x, sel_new, win_new = (pad_tok(a) for a in (q, gates, x, sel_new, win_new))
    p_n = page_table.shape[1]
    npg = min(SEL_PAGES_PER_STEP, p_n)
    nch = kcct.shape[2]
    nbuf = win_buf_t.shape[-1]
    rows = N_HEADS * TOK_PAD
    per_s = lambda shp: pl.BlockSpec((1,) + shp, lambda s, k, pt: (s, 0, 0))
    page_specs = [pl.BlockSpec((1, 1, 2, N_KV, HEAD_DIM, PAGE_SIZE),
                               functools.partial(_page_index, k=i, npg=npg, layer=layer, n_trail=4))
                  for i in range(npg)]
    buf_spec = pl.BlockSpec((1, 1, 2, N_KV, HEAD_DIM, nbuf), lambda s, k, pt: (layer, s, 0, 0, 0, 0))
    grid_spec = pltpu.PrefetchScalarGridSpec(
        num_scalar_prefetch=1,
        grid=(s_n, p_n // npg),
        in_specs=[per_s((TOK_PAD, Q_W)), per_s((TOK_PAD, N_GATE)), per_s((TOK_PAD, D_MODEL)),
                  per_s((KV_W, nch)), per_s((nch, KV_W)), per_s((TOK_PAD, 2 * KV_W)), per_s((TOK_PAD, 2 * KV_W)),
                  buf_spec] + page_specs +
                 [_const_spec(e_mat.shape), _const_spec(ovt.shape), _const_spec((Q_W, D_MODEL)),
                  _const_spec((1, D_MODEL))],
        out_specs=per_s((TOK_PAD, D_MODEL)),
        scratch_shapes=[pltpu.VMEM((rows, KV_W), BF16), pltpu.VMEM((rows, MAX_SEL_BLOCKS), BF16),
                        pltpu.VMEM((rows, 1), F32), pltpu.VMEM((rows, 1), F32), pltpu.VMEM((rows, KV_W), F32),
                        pltpu.VMEM((rows, KV_W), F32), pltpu.VMEM((1, 2, MAX_SEL_BLOCKS, MAX_SEL_BLOCKS), jnp.int32),
                        pltpu.VMEM((1, MAX_SEL_BLOCKS, MAX_SEL_BLOCKS), jnp.int32),
                        pltpu.VMEM((1, MAX_SEL_BLOCKS, MAX_SEL_BLOCKS), jnp.int32)],
    )
    return pl.pallas_call(
        functools.partial(_attn_sample_kernel, npg=npg, n_tok=n_tok, past=past),
        grid_spec=grid_spec,
        out_shape=jax.ShapeDtypeStruct((s_n, TOK_PAD, D_MODEL), F32),
        compiler_params=_cparams(2),
    )(page_table, q, gates, x, kcct, vcc, sel_new, win_new, win_buf_t, *([sel_pool_t] * npg), e_mat, ovt, wout,
      g3[None])[:, :n_tok]


def _conv_tail(y, lng_ref, lnb_ref, w2_ref, b2_ref, g3_ref):
    mean = jnp.mean(y, axis=-1, keepdims=True)
    yc = y - mean
    yn = yc * lax.rsqrt(jnp.mean(yc * yc, axis=-1, keepdims=True) + EPS) * lng_ref[...] + lnb_ref[...]
    act = (yn * _sigmoid(yn)).astype(BF16)
    return _rms(_dot(act, w2_ref[...]) + b2_ref[...], g3_ref[...])


def _glu(h, w1_ref, b1_ref):
    ag = _dot(h, w1_ref[...]) + b1_ref[...]
    d = ag.shape[1] // 2
    return ag[:, 0:d] * _sigmoid(ag[:, d:])


def _conv_prompt_kernel(h_ref, x_ref, w1_ref, b1_ref, wdw_ref, bdw_ref, lng_ref, lnb_ref, w2_ref, b2_ref, g3_ref,
                        o_ref, st_ref, ctx_scr, shift_scr):
    ti = pl.program_id(1)
    tm = h_ref.shape[1]

    @pl.when(ti == 0)
    def _():
        ctx_scr[0:CONV_HALO, :] = jnp.zeros((CONV_HALO, ctx_scr.shape[1]), F32)

    ctx_scr[CONV_HALO:CONV_HALO + tm, :] = _glu(h_ref[0], w1_ref, b1_ref)
    off = CONV_HALO - (CONV_WIDTH - 1)
    span = shift_scr.shape[1]
    for r in range(1, 8):
        shift_scr[r - 1] = ctx_scr[r:r + span, :]
    y = jnp.zeros((tm, ctx_scr.shape[1]), F32)
    for kk in range(CONV_WIDTH):
        a, r = divmod(off + kk, 8)
        tap = ctx_scr[8 * a:8 * a + tm, :] if r == 0 else shift_scr[r - 1, 8 * a:8 * a + tm, :]
        y = y + tap * wdw_ref[kk:kk + 1, :]
    y = y + bdw_ref[...]
    o_ref[0] = x_ref[0] + _conv_tail(y, lng_ref, lnb_ref, w2_ref, b2_ref, g3_ref)

    @pl.when(ti == pl.num_programs(1) - 1)
    def _():
        st_ref[0] = ctx_scr[tm + off:tm + CONV_HALO, :]

    ctx_scr[0:CONV_HALO, :] = ctx_scr[tm:tm + CONV_HALO, :]


def _conv_prompt(h, x, cv, g3):
    b, t, _ = h.shape
    d_in = cv["wdw"].shape[1]
    tm = min(TOKEN_TILE, t)
    row = pl.BlockSpec((1, tm, D_MODEL), lambda bi, ti: (bi, ti, 0))
    consts = [cv["w1"], cv["b1"], cv["wdw"], cv["bdw"], cv["lng"], cv["lnb"], cv["w2"], cv["b2"], g3[None]]
    return pl.pallas_call(
        _conv_prompt_kernel,
        grid=(b, t // tm),
        in_specs=[row, row] + [_const_spec(c.shape) for c in consts],
        out_specs=[row, pl.BlockSpec((1, CONV_WIDTH - 1, d_in), lambda bi, ti: (bi, 0, 0))],
        out_shape=[jax.ShapeDtypeStruct((b, t, D_MODEL), F32),
                   jax.ShapeDtypeStruct((b, CONV_WIDTH - 1, d_in), F32)],
        scratch_shapes=[pltpu.VMEM((tm + CONV_HALO, d_in), F32), pltpu.VMEM((7, tm + CONV_HALO - 8, d_in), F32)],
        compiler_params=_cparams(2),
    )(h, x, *consts)


def _conv_sample_kernel(h_ref, x_ref, st_ref, w1_ref, b1_ref, wdw_ref, bdw_ref, lng_ref, lnb_ref, w2_ref, b2_ref,
                        g3_ref, o_ref, sto_ref, *, n_tok, n_seq):
    u = _glu(h_ref[...], w1_ref, b1_ref)
    n_state = CONV_WIDTH - 1

    def ctx(i):
        return st_ref[i] if i < n_state else u[(i - n_state) * n_seq:(i - n_state + 1) * n_seq]

    ys = []
    for t in range(n_tok):
        y = jnp.zeros((n_seq, u.shape[1]), F32)
        for kk in range(CONV_WIDTH):
            y = y + ctx(t + kk) * wdw_ref[kk:kk + 1, :]
        ys.append(y)
    y = jnp.concatenate(ys, axis=0) + bdw_ref[...]
    o_ref[...] = x_ref[...] + _conv_tail(y, lng_ref, lnb_ref, w2_ref, b2_ref, g3_ref)
    for i in range(n_state):
        sto_ref[i] = ctx(i + n_tok)


def _conv_sample(h_ts, x_ts, state_t, cv, g3, n_tok):
    n, _ = h_ts.shape
    n_seq = n // n_tok
    d_in = cv["wdw"].shape[1]
    args = [h_ts, x_ts, state_t, cv["w1"], cv["b1"], cv["wdw"], cv["bdw"], cv["lng"], cv["lnb"], cv["w2"],
            cv["b2"], g3[None]]
    return pl.pallas_call(
        functools.partial(_conv_sample_kernel, n_tok=n_tok, n_seq=n_seq),
        grid=(1,),
        in_specs=[_const_spec(a.shape) for a in args],
        out_specs=[pl.BlockSpec((n, D_MODEL), lambda i: (0, 0)),
                   pl.BlockSpec((CONV_WIDTH - 1, n_seq, d_in), lambda i: (0, 0, 0))],
        out_shape=[jax.ShapeDtypeStruct((n, D_MODEL), F32),
                   jax.ShapeDtypeStruct((CONV_WIDTH - 1, n_seq, d_in), F32)],
        compiler_params=_cparams(1),
    )(*args)


def _rope_tables(pos):
    half = HEAD_DIM // 2
    inv = ROPE_THETA ** (-jnp.arange(half, dtype=F32) / half)
    ang = pos.astype(F32)[:, None] * inv[None, :]
    cos, sin = jnp.cos(ang), jnp.sin(ang)
    return jnp.concatenate([cos, cos, cos, cos], axis=1), jnp.concatenate([-sin, sin, -sin, sin], axis=1)


def _block_expand_matrix(n_keys):
    n = np.arange(MAX_SEL_BLOCKS)[:, None]
    k = np.arange(n_keys)[None, :]
    return jnp.asarray((k // SEL_BLOCK == n).astype(np.float32), dtype=BF16)


def _overlap_t(nch):
    n = np.arange(MAX_SEL_BLOCKS)[:, None] * SEL_BLOCK
    c = np.arange(nch)[None, :] * CMP_STRIDE
    ov = (c < n + SEL_BLOCK) & (c + CMP_LEN > n) & (np.arange(nch)[None, :] < nch - 1)
    return jnp.asarray(ov.astype(np.float32), dtype=BF16)


def _compress_weights(pe, w1, b1, w2, b2, nch):
    ratio = CMP_LEN // CMP_STRIDE
    eye = jnp.eye(N_KV, dtype=F32)
    w1r = w1.reshape(2, ratio, CMP_STRIDE, HEAD_DIM, HEAD_DIM)
    w1bd = jnp.einsum("gh,kride->krigdhe", eye, w1r).reshape(2, ratio, CMP_STRIDE, KV_W, KV_W).astype(BF16)
    w2bd = jnp.einsum("gh,kde->kgdhe", eye, w2).reshape(2, KV_W, KV_W).astype(BF16)
    cos, sin = _rope_tables(jnp.arange(nch) * CMP_STRIDE + CMP_LEN - 1)
    return {
        "w1": w1bd,
        "pe": jnp.tile(pe.reshape(2, ratio, CMP_STRIDE, HEAD_DIM), (1, 1, 1, N_KV)),
        "b1": jnp.tile(b1, (1, N_KV)),
        "w2": w2bd,
        "b2": jnp.tile(b2, (1, N_KV)),
        "cos": cos,
        "sin": sin,
    }


def _block_bias_matrix(n_keys):
    n = np.arange(MAX_SEL_BLOCKS)[:, None]
    k = np.arange(n_keys)[None, :]
    return jnp.asarray(np.where(k // SEL_BLOCK == n, NEG, 0.0).astype(np.float32), dtype=BF16)


def _rows_minor(a):
    nd = a.ndim
    return a.transpose(tuple(range(nd - 4)) + (nd - 3, nd - 2, nd - 1, nd - 4))


def _nsa_layer(hp, xp, hs, xs, cmp_pool_t, sel_pool_t, win_buf_t, win_buf, layer, page_table, w_in, w_out, pe, w1, b1,
               w2, b2, g3):
    b, t, _ = xp.shape
    s_n, n_tok, _ = xs.shape
    past = page_table.shape[1] * PAGE_SIZE
    assert t % TOKEN_TILE == 0 and t // SEL_BLOCK <= MAX_SEL_BLOCKS and t >= WINDOW + Q_BLOCK
    assert past // SEL_BLOCK == MAX_SEL_BLOCKS and n_tok <= TOK_PAD and past % SEL_BLOCK == 0
    wq = w_in[:, :Q_W].astype(BF16)
    wkv = w_in[:, Q_W:Q_W + 6 * KV_W].astype(BF16)
    wg = w_in[:, Q_W + 6 * KV_W:].astype(BF16)
    wout = w_out.astype(BF16)
    nch_p = t // CMP_STRIDE
    cw = _compress_weights(pe, w1, b1, w2, b2, nch_p)

    cos_p, sin_p = _rope_tables(jnp.arange(t))
    q, cmp_kv, sel_kv, win_kv, gates, kst, vsb, kwt, vwb = _proj(hp.reshape(b * t, D_MODEL), cos_p, sin_p, wq, wkv,
                                                                  wg, batch=b)
    ident_pages = jnp.arange(b * t // PAGE_SIZE, dtype=jnp.int32).reshape(b, t // PAGE_SIZE)
    kcct, vcc = _compress(_chunk_view(cmp_kv.reshape(1, b * t // PAGE_SIZE, PAGE_SIZE, 2 * KV_W)), 0, ident_pages, cw,
                          rows_minor=False)
    xp_new = _attn_prompt(q.reshape(b, t, Q_W), gates.reshape(b, t, N_GATE), xp, kcct, vcc, kst, vsb, kwt, vwb,
                          _block_bias_matrix(t), _overlap_t(nch_p), wout, g3)
    kv6 = lambda a, lead: a.reshape(lead + (2, N_KV, HEAD_DIM))
    n_win = min(WINDOW, t)
    outs_p = (kv6(cmp_kv, (b, t)), kv6(sel_kv, (b, t)), kv6(win_kv, (b, t))[:, t - n_win:])

    nch_s = past // CMP_STRIDE
    cw_s = cw if nch_s == nch_p else _compress_weights(pe, w1, b1, w2, b2, nch_s)
    cos_s, sin_s = _rope_tables(past + jnp.arange(s_n * n_tok) % n_tok)
    q_s, cmp_s, sel_s, win_s, gates_s = _proj(hs.reshape(s_n * n_tok, D_MODEL), cos_s, sin_s, wq, wkv, wg)
    kcct_s, vcc_s = _compress(cmp_pool_t, layer, page_table, cw_s, rows_minor=True)
    per_seq = lambda a: a.reshape(s_n, n_tok, a.shape[-1])
    nbuf = win_buf.shape[1]
    xs_new = _attn_sample(per_seq(q_s), per_seq(gates_s), xs, kcct_s, vcc_s, per_seq(sel_s), per_seq(win_s),
                          win_buf_t, sel_pool_t, layer, page_table, _block_expand_matrix(past), _overlap_t(nch_s),
                          wout, g3, past)
    win_all = jnp.concatenate([win_buf, kv6(win_s, (s_n, n_tok))], axis=1)[:, n_tok:]
    outs_s = (kv6(cmp_s, (s_n, n_tok)), kv6(sel_s, (s_n, n_tok)), win_all)
    return xp_new, xs_new, outs_p, outs_s


def _conv_layer(hp, xp, hs, xs, state, w_pw1, b_pw1, w_dw, b_dw, ln_g, ln_b, w_pw2, b_pw2, g3):
    s_n, n_tok, _ = xs.shape
    cv = {"w1": w_pw1.astype(BF16), "b1": b_pw1[None], "wdw": w_dw, "bdw": b_dw[None], "lng": ln_g[None],
          "lnb": ln_b[None], "w2": w_pw2.astype(BF16), "b2": b_pw2[None]}
    xp_new, st_p = _conv_prompt(hp, xp, cv, g3)
    to_ts = lambda a: a.reshape(s_n, n_tok, -1).transpose(1, 0, 2).reshape(s_n * n_tok, -1)
    xs_ts, st_t = _conv_sample(to_ts(hs), to_ts(xs), state.transpose(1, 0, 2), cv, g3, n_tok)
    xs_new = xs_ts.reshape(n_tok, s_n, -1).transpose(1, 0, 2)
    return xp_new, xs_new, st_p, st_t.transpose(1, 0, 2)


def kernel(x_prompt, x_sample, cache_cmp_kv, cache_sel_kv, state_win_kv, state_conv, page_table, norm_g, ffn_w_gu,
           ffn_w_down, attn_w_in, attn_w_out, cmp_pe, cmp_w1, cmp_b1, cmp_w2, cmp_b2, conv_w_pw1, conv_b_pw1,
           conv_w_dw, conv_b_dw, conv_ln_g, conv_ln_b, conv_w_pw2, conv_b_pw2):
    b, t, d = x_prompt.shape
    s_n, n_tok, _ = x_sample.shape
    depth = norm_g.shape[0]
    xp = x_prompt.reshape(b * t, d)
    xs = x_sample.reshape(s_n * n_tok, d)
    cmp_p, sel_p, win_p, conv_p = [], [], [], []
    cmp_s, sel_s, win_s, conv_s = [], [], [], []
    cmp_pool_t = _rows_minor(cache_cmp_kv)
    sel_pool_t = _rows_minor(cache_sel_kv)
    win_buf_t = _rows_minor(state_win_kv)
    for i in range(depth):
        g = norm_g[i]
        wgu, wdn = ffn_w_gu[i, 0].astype(BF16), ffn_w_down[i, 0].astype(BF16)
        xp, hp = _ffn(xp, g[0], g[1], g[2], wgu, wdn, True)
        xs, hs = _ffn(xs, g[0], g[1], g[2], wgu, wdn, True)
        j = i // 2
        xp3, xs3 = xp.reshape(b, t, d), xs.reshape(s_n, n_tok, d)
        hp3, hs3 = hp.reshape(b, t, d), hs.reshape(s_n, n_tok, d)
        if i % 2 == 0:
            xp3, xs3, o_p, o_s = _nsa_layer(hp3, xp3, hs3, xs3, cmp_pool_t, sel_pool_t, win_buf_t, state_win_kv[j], j,
                                            page_table, attn_w_in[j], attn_w_out[j], cmp_pe[j], cmp_w1[j], cmp_b1[j],
                                            cmp_w2[j], cmp_b2[j], g[3])
            cmp_p.append(o_p[0]); sel_p.append(o_p[1]); win_p.append(o_p[2])
            cmp_s.append(o_s[0]); sel_s.append(o_s[1]); win_s.append(o_s[2])
        else:
            xp3, xs3, st_p, st_s = _conv_layer(hp3, xp3, hs3, xs3, state_conv[j], conv_w_pw1[j], conv_b_pw1[j],
                                               conv_w_dw[j], conv_b_dw[j], conv_ln_g[j], conv_ln_b[j], conv_w_pw2[j],
                                               conv_b_pw2[j], g[3])
            conv_p.append(st_p); conv_s.append(st_s)
        xp, xs = xp3.reshape(b * t, d), xs3.reshape(s_n * n_tok, d)
        wgu, wdn = ffn_w_gu[i, 1].astype(BF16), ffn_w_down[i, 1].astype(BF16)
        xp, _ = _ffn(xp, g[4], g[5], g[5], wgu, wdn, False)
        xs, _ = _ffn(xs, g[4], g[5], g[5], wgu, wdn, False)
    return (xp.reshape(b, t, d), xs.reshape(s_n, n_tok, d), jnp.stack(cmp_p), jnp.stack(sel_p), jnp.stack(win_p),
            jnp.stack(conv_p), jnp.stack(cmp_s), jnp.stack(sel_s), jnp.stack(win_s), jnp.stack(conv_s))
```

```python
import functools

import numpy as np
import jax
import jax.numpy as jnp
from jax import lax
from jax.experimental import pallas as pl
from jax.experimental.pallas import tpu as pltpu

F32 = jnp.float32
BF16 = jnp.bfloat16

D_MODEL = 1024
N_HEADS = 16
HEAD_DIM = 64
N_KV = 4
GROUP = N_HEADS // N_KV
KV_W = N_KV * HEAD_DIM
Q_W = N_HEADS * HEAD_DIM
N_GATE = 3 * N_HEADS
CMP_LEN = 32
CMP_STRIDE = 16
SEL_BLOCK = 64
SEL_TOP = 16
WINDOW = 512
ROPE_THETA = 10000.0
CONV_WIDTH = 31
D_FF = 2816
EPS = 1e-6
NEG = -1e30
FORCED_KEY = 0x7F000000
LOG2E = 1.4426950408889634
Q_SCALE = HEAD_DIM ** -0.5 * LOG2E
PAGE_SIZE = 128

Q_BLOCK = 128
RANK_GROUPS_PER_LOOP = max(1, 256 // Q_BLOCK)
KEY_TILE = 1024
MAX_SEL_BLOCKS = 128
FFN_CHUNK = 256
TOKEN_TILE = 512
CONV_HALO = 32
CMP_PAGES_PER_STEP = 32
SEL_PAGES_PER_STEP = 32
VMEM_LIMIT_MB = 56


def _cparams(n_axes, vmem_mb=VMEM_LIMIT_MB):
    return pltpu.CompilerParams(dimension_semantics=("arbitrary",) * n_axes,
                                vmem_limit_bytes=vmem_mb * 1024 * 1024)


def _const_spec(shape):
    nd = len(shape)
    return pl.BlockSpec(shape, lambda *_: (0,) * nd, pipeline_mode=pl.Buffered(1))


def _dot(a, b):
    return jnp.dot(a, b, preferred_element_type=F32)


def _dot_nt(a, b):
    return lax.dot_general(a, b, (((1,), (1,)), ((), ())), preferred_element_type=F32)


def _rms(x, g):
    return x * lax.rsqrt(jnp.mean(x * x, axis=-1, keepdims=True) + EPS) * g


def _sigmoid(x):
    return 1.0 / (1.0 + jnp.exp(-x))


def _rope_lanes(x, cos, sin):
    out = []
    for c in range(x.shape[1] // 128):
        xc = x[:, c * 128:(c + 1) * 128]
        lane = lax.broadcasted_iota(jnp.int32, xc.shape, 1)
        first = (lane % HEAD_DIM) < (HEAD_DIM // 2)
        swapped = jnp.where(first, pltpu.roll(xc, 128 - HEAD_DIM // 2, 1), pltpu.roll(xc, HEAD_DIM // 2, 1))
        out.append(xc * cos + swapped * sin)
    return out[0] if len(out) == 1 else jnp.concatenate(out, axis=1)


def _masked_softmax2(s, mask):
    s = jnp.where(mask, s, NEG)
    m = jnp.max(s, axis=-1, keepdims=True)
    p = jnp.where(mask, jnp.exp2(s - m), 0.0)
    d = jnp.sum(p, axis=-1, keepdims=True)
    return p * (1.0 / jnp.where(d > 0, d, 1.0))


def _lane_place(x, shift):
    shift %= x.shape[1]
    return x if shift == 0 else pltpu.roll(x, shift, 1)


def _head_lane_mask(shape, slot):
    lane = lax.broadcasted_iota(jnp.int32, shape, 1)
    return (lane // HEAD_DIM) == slot


def _ffn_kernel(x_ref, gpre_ref, gpost_ref, gnext_ref, wgu_ref, wdn_ref, xo_ref, *maybe_h, emit_h):
    x = x_ref[...]
    hn = _rms(x, gpre_ref[...]).astype(BF16)
    acc = jnp.zeros(x.shape, F32)
    for c in range(D_FF // FFN_CHUNK):
        lo = c * FFN_CHUNK
        g = _dot(hn, wgu_ref[:, lo:lo + FFN_CHUNK])
        u = _dot(hn, wgu_ref[:, D_FF + lo:D_FF + lo + FFN_CHUNK])
        a = (g * _sigmoid(g) * u).astype(BF16)
        acc = acc + _dot(a, wdn_ref[lo:lo + FFN_CHUNK, :])
    y = x + 0.5 * _rms(acc, gpost_ref[...])
    xo_ref[...] = y
    if emit_h:
        maybe_h[0][...] = _rms(y, gnext_ref[...]).astype(BF16)


def _ffn(x, g_pre, g_post, g_next, wgu_all, wdn_all, layer, half, emit_h):
    n = x.shape[0]
    tm = min(TOKEN_TILE, n)
    row = pl.BlockSpec((tm, D_MODEL), lambda i: (i, 0))
    weight = lambda shape: pl.BlockSpec((None, None) + shape, lambda i: (layer, half, 0, 0),
                                        pipeline_mode=pl.Buffered(1))
    out_shape = [jax.ShapeDtypeStruct((n, D_MODEL), F32)]
    out_specs = [row]
    if emit_h:
        out_shape.append(jax.ShapeDtypeStruct((n, D_MODEL), BF16))
        out_specs.append(row)
    res = pl.pallas_call(
        functools.partial(_ffn_kernel, emit_h=emit_h),
        grid=(n // tm,),
        in_specs=[row, _const_spec((1, D_MODEL)), _const_spec((1, D_MODEL)), _const_spec((1, D_MODEL)),
                  weight((D_MODEL, 2 * D_FF)), weight((D_FF, D_MODEL))],
        out_specs=out_specs,
        out_shape=out_shape,
        compiler_params=_cparams(1),
    )(x, g_pre[None], g_post[None], g_next[None], wgu_all, wdn_all)
    return res if emit_h else (res[0], None)


def _proj_kernel(h_ref, cos_ref, sin_ref, wq_ref, wkv_ref, wg_ref, *refs, prompt, n_alias=0, tiles_per_seq=1):
    h = h_ref[...]
    cos = cos_ref[...]
    sin = sin_ref[...]
    q = (_rope_lanes(_dot(h, wq_ref[...]), cos, sin) * Q_SCALE).astype(BF16)
    kv = _dot(h, wkv_ref[...])
    ks = _rope_lanes(kv[:, 2 * KV_W:3 * KV_W], cos, sin)
    vs = kv[:, 3 * KV_W:4 * KV_W]
    kw = _rope_lanes(kv[:, 4 * KV_W:5 * KV_W], cos, sin)
    vw = kv[:, 5 * KV_W:6 * KV_W]
    gates = _sigmoid(_dot(h, wg_ref[...]))
    if not prompt:
        q_ref, cmp_ref, sel_ref, win_ref, gate_ref = refs
        q_ref[...] = q
        gate_ref[...] = gates
        cmp_ref[...] = kv[:, 0:2 * KV_W]
        sel_ref[:, 0:KV_W] = ks
        sel_ref[:, KV_W:2 * KV_W] = vs
        win_ref[:, 0:KV_W] = kw
        win_ref[:, KV_W:2 * KV_W] = vw
        return
    q_ref, gate_ref, cmpt_ref, selt_ref, win_ref, kst_ref, vs1_ref, kwt_ref, vw1_ref = refs[n_alias:]
    q_ref[...] = q
    gate_ref[...] = gates
    cmpt_ref[0, 0] = kv[:, 0:2 * KV_W].T
    ks_t = ks.T
    selt_ref[0, 0, 0:KV_W, :] = ks_t
    selt_ref[0, 0, KV_W:2 * KV_W, :] = vs.T
    kst_ref[0] = ks_t.astype(BF16)
    kwt_ref[0] = kw.T.astype(BF16)

    @pl.when(pl.program_id(0) % tiles_per_seq == tiles_per_seq - 1)
    def _():
        win_ref[0, :, 0:KV_W] = kw
        win_ref[0, :, KV_W:2 * KV_W] = vw

    for v, v1_ref in ((vs, vs1_ref), (vw, vw1_ref)):
        for g in range(N_KV):
            pair = v[:, (g // 2) * 128:(g // 2 + 1) * 128]
            if g % 2:
                pair = pltpu.roll(pair, HEAD_DIM, 1)
            lane = lax.broadcasted_iota(jnp.int32, pair.shape, 1)
            v1_ref[0, g] = jnp.where(lane < HEAD_DIM, pair, 1.0).astype(BF16)


def _proj_specs(h, cos):
    n = h.shape[0]
    tm = min(TOKEN_TILE, n)
    n_pos_tiles = cos.shape[0] // tm
    row = lambda w: pl.BlockSpec((tm, w), lambda i: (i, 0))
    tab = pl.BlockSpec((tm, 128), lambda i: (i % n_pos_tiles, 0))
    in_specs = [row(D_MODEL), tab, tab, _const_spec((D_MODEL, Q_W)), _const_spec((D_MODEL, 6 * KV_W)),
                _const_spec((D_MODEL, N_GATE))]
    return n, tm, row, in_specs


def _proj_sample(h, cos, sin, wq, wkv, wg):
    n, tm, row, in_specs = _proj_specs(h, cos)
    return pl.pallas_call(
        functools.partial(_proj_kernel, prompt=False),
        grid=(n // tm,),
        in_specs=in_specs,
        out_specs=[row(Q_W), row(2 * KV_W), row(2 * KV_W), row(2 * KV_W), row(N_GATE)],
        out_shape=[jax.ShapeDtypeStruct((n, Q_W), BF16), jax.ShapeDtypeStruct((n, 2 * KV_W), F32),
                   jax.ShapeDtypeStruct((n, 2 * KV_W), F32), jax.ShapeDtypeStruct((n, 2 * KV_W), F32),
                   jax.ShapeDtypeStruct((n, N_GATE), F32)],
        compiler_params=_cparams(1),
    )(h, cos, sin, wq, wkv, wg)


def _proj_prompt(h, cos, sin, wq, wkv, wg, batch, layer, n_layers, stacked=None):
    n, tm, row, in_specs = _proj_specs(h, cos)
    t = n // batch
    nt = t // tm
    assert tm == min(WINDOW, t)
    kvt_spec = pl.BlockSpec((1, 1, 2 * KV_W, tm), lambda i: (layer, i // nt, 0, i % nt))
    kt_spec = pl.BlockSpec((1, KV_W, tm), lambda i: (i // nt, 0, i % nt))
    v1_spec = pl.BlockSpec((1, N_KV, tm, 2 * HEAD_DIM), lambda i: (i // nt, 0, i % nt, 0))
    kvt_shape = jax.ShapeDtypeStruct((n_layers, batch, 2 * KV_W, t), F32)
    args = [h, cos, sin, wq, wkv, wg]
    aliases = {}
    if stacked is not None:
        args += list(stacked)
        in_specs = in_specs + [pl.BlockSpec(memory_space=pl.ANY)] * 2
        aliases = {6: 2, 7: 3}
    return pl.pallas_call(
        functools.partial(_proj_kernel, prompt=True, n_alias=len(aliases), tiles_per_seq=nt),
        grid=(n // tm,),
        in_specs=in_specs,
        out_specs=[row(Q_W), row(N_GATE), kvt_spec, kvt_spec,
                   pl.BlockSpec((1, tm, 2 * KV_W), lambda i: (i // nt, 0, 0)), kt_spec, v1_spec, kt_spec, v1_spec],
        out_shape=[jax.ShapeDtypeStruct((n, Q_W), BF16), jax.ShapeDtypeStruct((n, N_GATE), F32), kvt_shape, kvt_shape,
                   jax.ShapeDtypeStruct((batch, tm, 2 * KV_W), F32),
                   jax.ShapeDtypeStruct((batch, KV_W, t), BF16),
                   jax.ShapeDtypeStruct((batch, N_KV, t, 2 * HEAD_DIM), BF16),
                   jax.ShapeDtypeStruct((batch, KV_W, t), BF16),
                   jax.ShapeDtypeStruct((batch, N_KV, t, 2 * HEAD_DIM), BF16)],
        input_output_aliases=aliases,
        compiler_params=_cparams(1),
    )(*args)


def _compress_kernel(pt_ref, *refs, npg, nch):
    del pt_ref
    pages = refs[:npg]
    w1_ref, pe_ref, b1_ref, w2_ref, b2_ref, cos_ref, sin_ref, kt_ref, v_ref, part_ref, rows_scr = refs[npg:]
    j = pl.program_id(1)
    rows = npg * (PAGE_SIZE // CMP_STRIDE)
    row0 = pl.multiple_of(j * rows, rows)
    for pi, p in enumerate(pages):
        for kv in range(2):
            t = p[0, 0, kv].reshape(KV_W, PAGE_SIZE).T
            for h in range(KV_W // 128):
                rows_scr[kv, h, pi * PAGE_SIZE:(pi + 1) * PAGE_SIZE, :] = t[:, h * 128:(h + 1) * 128]
    for kv in range(2):
        acc = [jnp.zeros((rows, KV_W), F32) for _ in range(CMP_LEN // CMP_STRIDE)]
        for i in range(CMP_STRIDE):
            x = jnp.concatenate([rows_scr[kv, h, pl.ds(i, rows, stride=CMP_STRIDE), :]
                                 for h in range(KV_W // 128)], axis=1)
            for r in range(CMP_LEN // CMP_STRIDE):
                xr = (x + pe_ref[kv, r, i:i + 1, :]).astype(BF16)
                acc[r] = acc[r] + _dot(xr, w1_ref[kv, r, i])
        for r in range(CMP_LEN // CMP_STRIDE):
            part_ref[kv, pl.ds(row0, rows), r * KV_W:(r + 1) * KV_W] = acc[r]

    @pl.when(j == pl.num_programs(1) - 1)
    def _():
        for kv in range(2):
            hid = b1_ref[kv:kv + 1, :] + part_ref[kv, :, 0:KV_W]
            hid = hid + pltpu.roll(part_ref[kv, :, KV_W:2 * KV_W], nch - 1, 0)
            act = jax.nn.gelu(hid, approximate=True).astype(BF16)
            out = _dot(act, w2_ref[kv]) + b2_ref[kv:kv + 1, :]
            rowi = lax.broadcasted_iota(jnp.int32, out.shape, 0)
            out = jnp.where(rowi < nch - 1, out, 0.0)
            if kv == 0:
                out = _rope_lanes(out, cos_ref[...], sin_ref[...])
                kt_ref[0] = out.T.astype(BF16)
            else:
                v_ref[0] = out.astype(BF16)


def _page_index(s, j, pt, *, k, npg, layer, n_trail):
    return (layer, pt[s, j * npg + k]) + (0,) * n_trail


def _seq_page_index(s, j, pt, *, k, npg, layer):
    return (layer, s, 0, 0, 0, j * npg + k)


def _compress(pool, layer, page_table, cw, paged=True):
    s_n, p_n = page_table.shape
    chunks_per_page = PAGE_SIZE // CMP_STRIDE
    nch = p_n * chunks_per_page
    npg = min(CMP_PAGES_PER_STEP, p_n)
    if paged:
        page_block = pool.shape[2:]
        page_specs = [pl.BlockSpec((1, 1) + page_block,
                                   functools.partial(_page_index, k=k, npg=npg, layer=layer, n_trail=len(page_block)))
                      for k in range(npg)]
    else:
        page_specs = [pl.BlockSpec((1, 1, 2, N_KV, HEAD_DIM, PAGE_SIZE),
                                   functools.partial(_seq_page_index, k=k, npg=npg, layer=layer))
                      for k in range(npg)]
    consts = [cw["w1"], cw["pe"], cw["b1"], cw["w2"], cw["b2"], cw["cos"], cw["sin"]]
    scratch = [pltpu.VMEM((2, nch, 2 * KV_W), F32), pltpu.VMEM((2, KV_W // 128, npg * PAGE_SIZE, 128), F32)]
    grid_spec = pltpu.PrefetchScalarGridSpec(
        num_scalar_prefetch=1,
        grid=(s_n, p_n // npg),
        in_specs=page_specs + [_const_spec(c.shape) for c in consts],
        out_specs=[pl.BlockSpec((1, KV_W, nch), lambda s, j, pt: (s, 0, 0)),
                   pl.BlockSpec((1, nch, KV_W), lambda s, j, pt: (s, 0, 0))],
        scratch_shapes=scratch,
    )
    return pl.pallas_call(
        functools.partial(_compress_kernel, npg=npg, nch=nch),
        grid_spec=grid_spec,
        out_shape=[jax.ShapeDtypeStruct((s_n, KV_W, nch), BF16), jax.ShapeDtypeStruct((s_n, nch, KV_W), BF16)],
        compiler_params=_cparams(2),
    )(page_table, *([pool] * npg), *consts)


def _rank_values(imp_t, valid, forced):
    bits = pltpu.bitcast(imp_t, jnp.int32)
    return jnp.where(forced, FORCED_KEY, jnp.where(valid, bits, -1))


def _rank_counts(v_scr, cnt_scr, tie_scr, groups, n_rank_rows):
    n_blocks, lanes = v_scr.shape[2:]
    sub = lax.broadcasted_iota(jnp.int32, (8, lanes), 0)
    for g in groups:
        cnt_scr[g] = jnp.zeros((n_blocks, lanes), jnp.int32)
        tie_scr[g] = jnp.zeros((n_blocks, lanes), jnp.int32)

    def body(mb, carry):
        base = pl.multiple_of(mb * 8, 8)
        side = [jnp.where(r > mb, 1, 0) for r in range(n_blocks // 8)]
        for g in groups:
            own = v_scr[g, 0, pl.ds(base, 8), :]
            ahead = [jnp.zeros((8, lanes), jnp.int32) for _ in range(n_blocks // 8)]
            ties = jnp.zeros((8, lanes), jnp.int32)
            for mi in range(8):
                row = v_scr[g, 0, pl.ds(base + mi, 1), :]
                for r in range(n_blocks // 8):
                    ahead[r] = ahead[r] + jnp.where(row > v_scr[g, side[r], r * 8:(r + 1) * 8, :], 1, 0)
                ties = ties + jnp.where((row == own) & (sub > mi), 1, 0)
            cnt_scr[g] += jnp.concatenate(ahead, axis=0)
            tie_scr[g, pl.ds(base, 8), :] += ties
        return carry

    lax.fori_loop(0, (n_rank_rows + 7) // 8, body, 0)
    return tuple(cnt_scr[g] + tie_scr[g] for g in groups)


def _importance_t(p_sum, ovt_ref):
    hi = p_sum.astype(BF16)
    lo = (p_sum - hi.astype(F32)).astype(BF16)
    return _dot_nt(ovt_ref[...], hi) + _dot_nt(ovt_ref[...], lo)


SELNEG_LANES = MAX_SEL_BLOCKS
Q_LANE0 = SELNEG_LANES


def _key_operand(mask_rows, k_rows):
    n = k_rows.shape[1]
    top = jnp.zeros((SELNEG_LANES, n), BF16) if mask_rows is None else mask_rows
    return jnp.concatenate([top, k_rows, jnp.zeros((KV_W - SELNEG_LANES - HEAD_DIM, n), BF16)], axis=0)


def _ones_normalise(acc):
    lane = lax.broadcasted_iota(jnp.int32, acc.shape, 1)
    return jnp.where(lane < HEAD_DIM, acc * (1.0 / pltpu.roll(acc, HEAD_DIM, 1)), 0.0)


def _biased(s, bias):
    n = s.shape[1]
    return (s.reshape(GROUP, Q_BLOCK, n) + bias[None]).reshape(s.shape)


def _attn_prompt_kernel(q_ref, gate_ref, x_ref, kcct_ref, vcc_ref, kst_ref, vs1_ref, kwt_ref, vw1_ref,
                        eneg_ref, ovt_ref, wout_ref, g3_ref, o_ref, qm_scr, oc_scr, ow_scr, v_scr, cnt_scr, tie_scr, acc_scr):
    qb = pl.program_id(1)
    q0 = qb * Q_BLOCK
    rows = GROUP * Q_BLOCK
    nch = kcct_ref.shape[2]
    heads = lambda g: slice(g * HEAD_DIM, (g + 1) * HEAD_DIM)
    tok_pos = lambda n: q0 + lax.broadcasted_iota(jnp.int32, (Q_BLOCK, n), 0)
    key_idx = lambda n: lax.broadcasted_iota(jnp.int32, (Q_BLOCK, n), 1)

    c_bias = jnp.where(key_idx(nch) * CMP_STRIDE + (CMP_LEN - 1) <= tok_pos(nch), 0.0, NEG)
    wlen = WINDOW + Q_BLOCK
    w0 = pl.multiple_of(jnp.maximum(q0 - WINDOW, 0), Q_BLOCK)
    w_key = w0 + key_idx(wlen)
    w_bias = jnp.where((w_key <= tok_pos(wlen)) & (w_key > tok_pos(wlen) - WINDOW), 0.0, NEG)
    n_idx = lax.broadcasted_iota(jnp.int32, (MAX_SEL_BLOCKS, Q_BLOCK), 0)
    cur = (q0 + lax.broadcasted_iota(jnp.int32, (MAX_SEL_BLOCKS, Q_BLOCK), 1)) // SEL_BLOCK
    valid = n_idx <= cur
    forced = valid & ((n_idx == 0) | (n_idx >= cur - 1))
    for g in range(N_KV):
        qg = q_ref[0, :, g * KV_W:(g + 1) * KV_W].astype(F32)
        q_lanes = _head_lane_mask(qg.shape, Q_LANE0 // HEAD_DIM)
        qm = jnp.concatenate(
            [jnp.where(q_lanes, _lane_place(qg, Q_LANE0 - j * HEAD_DIM), 0.0) for j in range(GROUP)], axis=0
        ).astype(BF16)
        qm_scr[g] = qm
        s_c = _biased(_dot(qm, _key_operand(None, kcct_ref[0, heads(g), :])), c_bias)
        m_c = jnp.max(s_c, axis=-1, keepdims=True)
        p_c = jnp.exp2(s_c - m_c)
        d_c = jnp.sum(p_c, axis=-1, keepdims=True)
        p_c = p_c * jnp.where(m_c > 0.5 * NEG, 1.0 / d_c, 0.0)
        o_c = _dot(p_c.astype(BF16), vcc_ref[0])
        o_c = o_c[:, (g // 2) * 128:(g // 2 + 1) * 128]
        oc_scr[g] = pltpu.roll(o_c, HEAD_DIM, 1) if g % 2 else o_c
        p_sum = p_c[0:Q_BLOCK]
        for j in range(1, GROUP):
            p_sum = p_sum + p_c[j * Q_BLOCK:(j + 1) * Q_BLOCK]
        keys = _rank_values(_importance_t(p_sum, ovt_ref), valid, forced)
        v_scr[g, 0] = keys
        v_scr[g, 1] = keys - 1
        s_w = _biased(_dot(qm, _key_operand(None, kwt_ref[0, heads(g), pl.ds(w0, wlen)])), w_bias)
        p_w = jnp.exp2(s_w - jnp.max(s_w, axis=-1, keepdims=True)).astype(BF16)
        ow_scr[g] = _ones_normalise(_dot(p_w, vw1_ref[0, g, pl.ds(w0, wlen), :]))

    n_live = jnp.minimum((q0 + Q_BLOCK) // SEL_BLOCK, MAX_SEL_BLOCKS)
    for g0 in range(0, N_KV, RANK_GROUPS_PER_LOOP):
        batch = tuple(range(g0, g0 + RANK_GROUPS_PER_LOOP))
        cnts = _rank_counts(v_scr, cnt_scr, tie_scr, batch, n_live)
        for g, cnt in zip(batch, cnts):
            not_picked = jnp.where(valid & (cnt < SEL_TOP), 0.0, 1.0).T.astype(BF16)
            for j in range(GROUP):
                qm_scr[g, j * Q_BLOCK:(j + 1) * Q_BLOCK, 0:SELNEG_LANES] = not_picked

    def sel_tile(kt, ms):
        k0 = pl.multiple_of(kt * KEY_TILE, KEY_TILE)
        blk_bias = eneg_ref[:, pl.ds(k0, KEY_TILE)]
        tri = jnp.where(k0 + key_idx(KEY_TILE) <= tok_pos(KEY_TILE), 0.0, NEG)
        out = []
        for g in range(N_KV):
            s = _dot(qm_scr[g], _key_operand(blk_bias, kst_ref[0, heads(g), pl.ds(k0, KEY_TILE)]))
            s = _biased(s, tri)
            m_new = jnp.maximum(ms[g], jnp.max(s, axis=-1, keepdims=True))
            alpha = jnp.exp2(ms[g] - m_new)
            p = jnp.exp2(s - m_new).astype(BF16)
            acc_scr[g] = alpha * acc_scr[g] + _dot(p, vs1_ref[0, g, pl.ds(k0, KEY_TILE), :])
            out.append(m_new)
        return tuple(out)

    acc_scr[...] = jnp.zeros(acc_scr.shape, F32)
    n_tiles = (q0 + Q_BLOCK + KEY_TILE - 1) // KEY_TILE
    lax.fori_loop(0, n_tiles, sel_tile, tuple(jnp.full((rows, 1), NEG, F32) for _ in range(N_KV)))

    slabs = []
    for g in range(N_KV):
        o_w = ow_scr[g]
        o_s = _ones_normalise(acc_scr[g])
        o_c = oc_scr[g]
        halves = [jnp.zeros((Q_BLOCK, 2 * HEAD_DIM), F32) for _ in range(GROUP // 2)]
        for j in range(GROUP):
            c = (g * GROUP + j) * 3
            rs = slice(j * Q_BLOCK, (j + 1) * Q_BLOCK)
            mix = (gate_ref[0, :, c:c + 1] * o_c[rs] + gate_ref[0, :, c + 1:c + 2] * o_s[rs]
                   + gate_ref[0, :, c + 2:c + 3] * o_w[rs])
            mix = jnp.where(_head_lane_mask(mix.shape, j % 2), _lane_place(mix, (j % 2) * HEAD_DIM), 0.0)
            halves[j // 2] = halves[j // 2] + mix
        slabs += halves
    o = jnp.concatenate(slabs, axis=1).astype(BF16)
    o_ref[0] = x_ref[0] + _rms(_dot(o, wout_ref[...]), g3_ref[...])


def _attn_prompt(q, gates, x, kcct, vcc, kst, vs1, kwt, vw1, eneg, ovt, wout, g3):
    b, t, _ = q.shape
    nch = kcct.shape[2]
    rows = GROUP * Q_BLOCK
    per_q = lambda w: pl.BlockSpec((1, Q_BLOCK, w), lambda bi, qi: (bi, qi, 0))
    per_b = lambda s: pl.BlockSpec((1,) + s, lambda bi, qi: (bi,) + (0,) * len(s), pipeline_mode=pl.Buffered(1))
    return pl.pallas_call(
        _attn_prompt_kernel,
        grid=(b, t // Q_BLOCK),
        in_specs=[per_q(Q_W), per_q(N_GATE), per_q(D_MODEL),
                  per_b((KV_W, nch)), per_b((nch, KV_W)),
                  per_b((KV_W, t)), per_b((N_KV, t, 2 * HEAD_DIM)), per_b((KV_W, t)), per_b((N_KV, t, 2 * HEAD_DIM)),
                  _const_spec(eneg.shape), _const_spec(ovt.shape), _const_spec((Q_W, D_MODEL)),
                  _const_spec((1, D_MODEL))],
        out_specs=per_q(D_MODEL),
        out_shape=jax.ShapeDtypeStruct((b, t, D_MODEL), F32),
        scratch_shapes=[pltpu.VMEM((N_KV, rows, KV_W), BF16), pltpu.VMEM((N_KV, rows, 2 * HEAD_DIM), F32),
                        pltpu.VMEM((N_KV, rows, 2 * HEAD_DIM), F32),
                        pltpu.VMEM((N_KV, 2, MAX_SEL_BLOCKS, Q_BLOCK), jnp.int32),
                        pltpu.VMEM((N_KV, MAX_SEL_BLOCKS, Q_BLOCK), jnp.int32),
                        pltpu.VMEM((N_KV, MAX_SEL_BLOCKS, Q_BLOCK), jnp.int32),
                        pltpu.VMEM((N_KV, rows, 2 * HEAD_DIM), F32)],
        compiler_params=_cparams(2),
    )(q, gates, x, kcct, vcc, kst, vs1, kwt, vw1, eneg, ovt, wout, g3[None])


TOK_PAD = 8


def _attn_sample_kernel(pt_ref, q_ref, gate_ref, x_ref, kcct_ref, vcc_ref, selnew_ref, winnew_ref, winbuf_ref,
                        *refs, npg, n_tok, past):
    del pt_ref
    pages = refs[:npg]
    (e_ref, ovt_ref, wout_ref, g3_ref, o_ref,
     qexp_scr, selrow_scr, m_scr, l_scr, acc_scr, oc_scr, v_scr, cnt_scr, tie_scr) = refs[npg:]
    k = pl.program_id(1)
    rows = N_HEADS * TOK_PAD
    n_cache_blocks = past // SEL_BLOCK

    def row_tok(shape):
        return lax.broadcasted_iota(jnp.int32, shape, 0) % TOK_PAD

    def pad_rows(a, n):
        return jnp.concatenate([a, jnp.zeros((n - a.shape[0], a.shape[1]), a.dtype)], axis=0)

    @pl.when(k == 0)
    def _():
        q8 = q_ref[0].astype(F32)
        pieces = []
        for g in range(N_KV):
            qg = q8[:, g * KV_W:(g + 1) * KV_W]
            g_lanes = _head_lane_mask(qg.shape, g)
            for j in range(GROUP):
                pieces.append(jnp.where(g_lanes, _lane_place(qg, (g - j) * HEAD_DIM), 0.0))
        qexp = jnp.concatenate(pieces, axis=0).astype(BF16)
        qexp_scr[...] = qexp

        s_c = _dot(qexp, kcct_ref[0])
        c_end = lax.broadcasted_iota(jnp.int32, s_c.shape, 1) * CMP_STRIDE + (CMP_LEN - 1)
        p_c = _masked_softmax2(s_c, c_end <= past + row_tok(s_c.shape))
        oc_scr[...] = _dot(p_c.astype(BF16), vcc_ref[0])

        sums = []
        for g in range(N_KV):
            acc = p_c[(g * GROUP) * TOK_PAD:(g * GROUP + 1) * TOK_PAD]
            for j in range(1, GROUP):
                acc = acc + p_c[(g * GROUP + j) * TOK_PAD:(g * GROUP + j + 1) * TOK_PAD]
            sums.append(acc)
        p_sum = pad_rows(jnp.concatenate(sums, axis=0), MAX_SEL_BLOCKS)
        imp_t = _importance_t(p_sum, ovt_ref)
        n_idx = lax.broadcasted_iota(jnp.int32, imp_t.shape, 0)
        cur = (past + lax.broadcasted_iota(jnp.int32, imp_t.shape, 1) % TOK_PAD) // SEL_BLOCK
        valid = n_idx <= cur
        forced = valid & ((n_idx == 0) | (n_idx >= cur - 1))
        keys = _rank_values(imp_t, valid, forced)
        v_scr[0, 0] = keys
        v_scr[0, 1] = keys - 1
        (cnt,) = _rank_counts(v_scr, cnt_scr, tie_scr, (0,), n_cache_blocks)
        sel = jnp.where(valid & (cnt < SEL_TOP - 1), 1.0, 0.0).T
        sel_rows = []
        for g in range(N_KV):
            sel_rows += [sel[g * TOK_PAD:(g + 1) * TOK_PAD]] * GROUP
        selrow_scr[...] = jnp.concatenate(sel_rows, axis=0).astype(BF16)
        m_scr[...] = jnp.full(m_scr.shape, NEG, F32)
        l_scr[...] = jnp.zeros(l_scr.shape, F32)
        acc_scr[...] = jnp.zeros(acc_scr.shape, F32)

    def online_update(s, v_tiles):
        m_old = m_scr[...]
        m_new = jnp.maximum(m_old, jnp.max(s, axis=-1, keepdims=True))
        alpha = jnp.exp2(m_old - m_new)
        p = jnp.exp2(s - m_new)
        l_scr[...] = alpha * l_scr[...] + jnp.sum(p, axis=-1, keepdims=True)
        acc = alpha * acc_scr[...]
        w = s.shape[1] // len(v_tiles)
        for i, (v, transposed) in enumerate(v_tiles):
            pv = (_dot_nt if transposed else _dot)
            acc = acc + pv(p[:, i * w:(i + 1) * w].astype(BF16), v)
        acc_scr[...] = acc
        m_scr[...] = m_new

    page_t = lambda p, kv: p[0, 0, kv].reshape(KV_W, PAGE_SIZE).astype(BF16)
    qexp = qexp_scr[...]
    keys = npg * PAGE_SIZE
    k0 = pl.multiple_of(k * keys, keys)
    s = jnp.concatenate([_dot(qexp, page_t(p, 0)) for p in pages], axis=1)
    picked = _dot(selrow_scr[...], e_ref[:, pl.ds(k0, keys)])
    s = s + jnp.where(picked > 0.5, 0.0, NEG)
    online_update(s, [(page_t(p, 1), True) for p in pages])

    @pl.when(k == pl.num_programs(1) - 1)
    def _():
        tok = row_tok((rows, PAGE_SIZE))
        new_i = lax.broadcasted_iota(jnp.int32, (rows, PAGE_SIZE), 1)
        new_ok = (new_i < n_tok) & (new_i <= tok)

        sel_new = pad_rows(selnew_ref[0], PAGE_SIZE)
        s_n = _dot_nt(qexp, sel_new[:, 0:KV_W].astype(BF16))
        online_update(jnp.where(new_ok, s_n, NEG), [(sel_new[:, KV_W:2 * KV_W].astype(BF16), False)])
        o_s = acc_scr[...] * (1.0 / l_scr[...])

        nbuf = winbuf_ref.shape[-1]
        buf_t = lambda kv: winbuf_ref[0, 0, kv].reshape(KV_W, nbuf).astype(BF16)
        win_new = pad_rows(winnew_ref[0], PAGE_SIZE)
        s_b = _dot(qexp, buf_t(0))
        s_nw = _dot_nt(qexp, win_new[:, 0:KV_W].astype(BF16))
        s_w = jnp.concatenate([s_b, s_nw], axis=1)
        col = lax.broadcasted_iota(jnp.int32, s_w.shape, 1)
        tok_w = row_tok(s_w.shape)
        in_buf = (col < nbuf) & (past - nbuf + col > past + tok_w - WINDOW) & (past - nbuf + col >= 0)
        in_new = (col >= nbuf) & (col - nbuf < n_tok) & (col - nbuf <= tok_w)
        p_w = _masked_softmax2(s_w, in_buf | in_new)
        o_w = (_dot_nt(p_w[:, 0:nbuf].astype(BF16), buf_t(1))
               + _dot(p_w[:, nbuf:].astype(BF16), win_new[:, KV_W:2 * KV_W].astype(BF16)))

        o_c = oc_scr[...]
        gates = gate_ref[0]
        slabs = []
        for g in range(N_KV):
            slab = jnp.zeros((TOK_PAD, KV_W), F32)
            for j in range(GROUP):
                c = (g * GROUP + j) * 3
                rs = slice((g * GROUP + j) * TOK_PAD, (g * GROUP + j + 1) * TOK_PAD)
                mix = gates[:, c:c + 1] * o_c[rs] + gates[:, c + 1:c + 2] * o_s[rs] + gates[:, c + 2:c + 3] * o_w[rs]
                slab = slab + jnp.where(_head_lane_mask(mix.shape, j), _lane_place(mix, (j - g) * HEAD_DIM), 0.0)
            slabs.append(slab)
        o = jnp.concatenate(slabs, axis=1).astype(BF16)
        y = _rms(_dot(o, wout_ref[...]), g3_ref[...])
        o_ref[0] = x_ref[0] + y


def _attn_sample(q, gates, x, kcct, vcc, sel_new, win_new, win_buf_t, sel_pool_t, layer, page_table, e_mat, ovt,
                 wout, g3, past):
    s_n, n_tok, _ = q.shape
    pad_tok = lambda a: jnp.pad(a, ((0, 0), (0, TOK_PAD - n_tok), (0, 0)))
    q, gates, x, sel_new, win_new = (pad_tok(a) for a in (q, gates, x, sel_new, win_new))
    p_n = page_table.shape[1]
    npg = min(SEL_PAGES_PER_STEP, p_n)
    nch = kcct.shape[2]
    nbuf = win_buf_t.shape[-1]
    rows = N_HEADS * TOK_PAD
    per_s = lambda shp: pl.BlockSpec((1,) + shp, lambda s, k, pt: (s, 0, 0))
    page_specs = [pl.BlockSpec((1, 1, 2, N_KV, HEAD_DIM, PAGE_SIZE),
                               functools.partial(_page_index, k=i, npg=npg, layer=layer, n_trail=4))
                  for i in range(npg)]
    buf_spec = pl.BlockSpec((1, 1, 2, N_KV, HEAD_DIM, nbuf), lambda s, k, pt: (layer, s, 0, 0, 0, 0))
    grid_spec = pltpu.PrefetchScalarGridSpec(
        num_scalar_prefetch=1,
        grid=(s_n, p_n // npg),
        in_specs=[per_s((TOK_PAD, Q_W)), per_s((TOK_PAD, N_GATE)), per_s((TOK_PAD, D_MODEL)),
                  per_s((KV_W, nch)), per_s((nch, KV_W)), per_s((TOK_PAD, 2 * KV_W)), per_s((TOK_PAD, 2 * KV_W)),
                  buf_spec] + page_specs +
                 [_const_spec(e_mat.shape), _const_spec(ovt.shape), _const_spec((Q_W, D_MODEL)),
                  _const_spec((1, D_MODEL))],
        out_specs=per_s((TOK_PAD, D_MODEL)),
        scratch_shapes=[pltpu.VMEM((rows, KV_W), BF16), pltpu.VMEM((rows, MAX_SEL_BLOCKS), BF16),
                        pltpu.VMEM((rows, 1), F32), pltpu.VMEM((rows, 1), F32), pltpu.VMEM((rows, KV_W), F32),
                        pltpu.VMEM((rows, KV_W), F32), pltpu.VMEM((1, 2, MAX_SEL_BLOCKS, MAX_SEL_BLOCKS), jnp.int32),
                        pltpu.VMEM((1, MAX_SEL_BLOCKS, MAX_SEL_BLOCKS), jnp.int32),
                        pltpu.VMEM((1, MAX_SEL_BLOCKS, MAX_SEL_BLOCKS), jnp.int32)],
    )
    return pl.pallas_call(
        functools.partial(_attn_sample_kernel, npg=npg, n_tok=n_tok, past=past),
        grid_spec=grid_spec,
        out_shape=jax.ShapeDtypeStruct((s_n, TOK_PAD, D_MODEL), F32),
        compiler_params=_cparams(2),
    )(page_table, q, gates, x, kcct, vcc, sel_new, win_new, win_buf_t, *([sel_pool_t] * npg), e_mat, ovt, wout,
      g3[None])[:, :n_tok]


def _conv_tail(y, lng_ref, lnb_ref, w2_ref, b2_ref, g3_ref):
    mean = jnp.mean(y, axis=-1, keepdims=True)
    yc = y - mean
    yn = yc * lax.rsqrt(jnp.mean(yc * yc, axis=-1, keepdims=True) + EPS) * lng_ref[...] + lnb_ref[...]
    act = (yn * _sigmoid(yn)).astype(BF16)
    return _rms(_dot(act, w2_ref[...]) + b2_ref[...], g3_ref[...])


def _glu(h, w1_ref, b1_ref):
    ag = _dot(h, w1_ref[...]) + b1_ref[...]
    d = ag.shape[1] // 2
    return ag[:, 0:d] * _sigmoid(ag[:, d:])


def _conv_prompt_kernel(h_ref, x_ref, w1_ref, b1_ref, wdw_ref, bdw_ref, lng_ref, lnb_ref, w2_ref, b2_ref, g3_ref,
                        o_ref, st_ref, ctx_scr, shift_scr):
    ti = pl.program_id(1)
    tm = h_ref.shape[1]

    @pl.when(ti == 0)
    def _():
        ctx_scr[0:CONV_HALO, :] = jnp.zeros((CONV_HALO, ctx_scr.shape[1]), F32)

    ctx_scr[CONV_HALO:CONV_HALO + tm, :] = _glu(h_ref[0], w1_ref, b1_ref)
    off = CONV_HALO - (CONV_WIDTH - 1)
    span = shift_scr.shape[1]
    for r in range(1, 8):
        shift_scr[r - 1] = ctx_scr[r:r + span, :]
    y = jnp.zeros((tm, ctx_scr.shape[1]), F32)
    for kk in range(CONV_WIDTH):
        a, r = divmod(off + kk, 8)
        tap = ctx_scr[8 * a:8 * a + tm, :] if r == 0 else shift_scr[r - 1, 8 * a:8 * a + tm, :]
        y = y + tap * wdw_ref[kk:kk + 1, :]
    y = y + bdw_ref[...]
    o_ref[0] = x_ref[0] + _conv_tail(y, lng_ref, lnb_ref, w2_ref, b2_ref, g3_ref)

    @pl.when(ti == pl.num_programs(1) - 1)
    def _():
        st_ref[0] = ctx_scr[tm + off:tm + CONV_HALO, :]

    ctx_scr[0:CONV_HALO, :] = ctx_scr[tm:tm + CONV_HALO, :]


def _conv_prompt(h, x, cv, g3):
    b, t, _ = h.shape
    d_in = cv["wdw"].shape[1]
    tm = min(TOKEN_TILE, t)
    row = pl.BlockSpec((1, tm, D_MODEL), lambda bi, ti: (bi, ti, 0))
    consts = [cv["w1"], cv["b1"], cv["wdw"], cv["bdw"], cv["lng"], cv["lnb"], cv["w2"], cv["b2"], g3[None]]
    return pl.pallas_call(
        _conv_prompt_kernel,
        grid=(b, t // tm),
        in_specs=[row, row] + [_const_spec(c.shape) for c in consts],
        out_specs=[row, pl.BlockSpec((1, CONV_WIDTH - 1, d_in), lambda bi, ti: (bi, 0, 0))],
        out_shape=[jax.ShapeDtypeStruct((b, t, D_MODEL), F32),
                   jax.ShapeDtypeStruct((b, CONV_WIDTH - 1, d_in), F32)],
        scratch_shapes=[pltpu.VMEM((tm + CONV_HALO, d_in), F32), pltpu.VMEM((7, tm + CONV_HALO - 8, d_in), F32)],
        compiler_params=_cparams(2),
    )(h, x, *consts)


def _conv_sample_kernel(h_ref, x_ref, st_ref, w1_ref, b1_ref, wdw_ref, bdw_ref, lng_ref, lnb_ref, w2_ref, b2_ref,
                        g3_ref, o_ref, sto_ref, *, n_tok, n_seq):
    u = _glu(h_ref[...], w1_ref, b1_ref)
    n_state = CONV_WIDTH - 1

    def ctx(i):
        return st_ref[i] if i < n_state else u[(i - n_state) * n_seq:(i - n_state + 1) * n_seq]

    ys = []
    for t in range(n_tok):
        y = jnp.zeros((n_seq, u.shape[1]), F32)
        for kk in range(CONV_WIDTH):
            y = y + ctx(t + kk) * wdw_ref[kk:kk + 1, :]
        ys.append(y)
    y = jnp.concatenate(ys, axis=0) + bdw_ref[...]
    o_ref[...] = x_ref[...] + _conv_tail(y, lng_ref, lnb_ref, w2_ref, b2_ref, g3_ref)
    for i in range(n_state):
        sto_ref[i] = ctx(i + n_tok)


def _conv_sample(h_ts, x_ts, state_t, cv, g3, n_tok):
    n, _ = h_ts.shape
    n_seq = n // n_tok
    d_in = cv["wdw"].shape[1]
    args = [h_ts, x_ts, state_t, cv["w1"], cv["b1"], cv["wdw"], cv["bdw"], cv["lng"], cv["lnb"], cv["w2"],
            cv["b2"], g3[None]]
    return pl.pallas_call(
        functools.partial(_conv_sample_kernel, n_tok=n_tok, n_seq=n_seq),
        grid=(1,),
        in_specs=[_const_spec(a.shape) for a in args],
        out_specs=[pl.BlockSpec((n, D_MODEL), lambda i: (0, 0)),
                   pl.BlockSpec((CONV_WIDTH - 1, n_seq, d_in), lambda i: (0, 0, 0))],
        out_shape=[jax.ShapeDtypeStruct((n, D_MODEL), F32),
                   jax.ShapeDtypeStruct((CONV_WIDTH - 1, n_seq, d_in), F32)],
        compiler_params=_cparams(1),
    )(*args)


def _rope_tables(pos):
    half = HEAD_DIM // 2
    inv = ROPE_THETA ** (-jnp.arange(half, dtype=F32) / half)
    ang = pos.astype(F32)[:, None] * inv[None, :]
    cos, sin = jnp.cos(ang), jnp.sin(ang)
    return jnp.concatenate([cos, cos, cos, cos], axis=1), jnp.concatenate([-sin, sin, -sin, sin], axis=1)


def _block_expand_matrix(n_keys):
    n = np.arange(MAX_SEL_BLOCKS)[:, None]
    k = np.arange(n_keys)[None, :]
    return jnp.asarray((k // SEL_BLOCK == n).astype(np.float32), dtype=BF16)


def _overlap_t(nch):
    n = np.arange(MAX_SEL_BLOCKS)[:, None] * SEL_BLOCK
    c = np.arange(nch)[None, :] * CMP_STRIDE
    ov = (c < n + SEL_BLOCK) & (c + CMP_LEN > n) & (np.arange(nch)[None, :] < nch - 1)
    return jnp.asarray(ov.astype(np.float32), dtype=BF16)


def _compress_weights(pe, w1, b1, w2, b2, nch):
    ratio = CMP_LEN // CMP_STRIDE
    eye = jnp.eye(N_KV, dtype=F32)
    w1r = w1.reshape(2, ratio, CMP_STRIDE, HEAD_DIM, HEAD_DIM)
    w1bd = jnp.einsum("gh,kride->krigdhe", eye, w1r).reshape(2, ratio, CMP_STRIDE, KV_W, KV_W).astype(BF16)
    w2bd = jnp.einsum("gh,kde->kgdhe", eye, w2).reshape(2, KV_W, KV_W).astype(BF16)
    cos, sin = _rope_tables(jnp.arange(nch) * CMP_STRIDE + CMP_LEN - 1)
    return {
        "w1": w1bd,
        "pe": jnp.tile(pe.reshape(2, ratio, CMP_STRIDE, HEAD_DIM), (1, 1, 1, N_KV)),
        "b1": jnp.tile(b1, (1, N_KV)),
        "w2": w2bd,
        "b2": jnp.tile(b2, (1, N_KV)),
        "cos": cos,
        "sin": sin,
    }


def _block_bias_matrix(n_keys):
    n = np.arange(MAX_SEL_BLOCKS)[:, None]
    k = np.arange(n_keys)[None, :]
    return jnp.asarray(np.where(k // SEL_BLOCK == n, NEG, 0.0).astype(np.float32), dtype=BF16)


def _rows_minor(a):
    nd = a.ndim
    return a.transpose(tuple(range(nd - 4)) + (nd - 3, nd - 2, nd - 1, nd - 4))


def _nsa_layer(hp, xp, hs, xs, cmp_pool_t, sel_pool_t, win_buf_t, win_buf, layer, n_layers, stacked, page_table,
               w_in, w_out, pe, w1, b1, w2, b2, g3):
    b, t, _ = xp.shape
    s_n, n_tok, _ = xs.shape
    past = page_table.shape[1] * PAGE_SIZE
    assert t % TOKEN_TILE == 0 and t // SEL_BLOCK <= MAX_SEL_BLOCKS and t >= WINDOW + Q_BLOCK
    assert past // SEL_BLOCK == MAX_SEL_BLOCKS and n_tok <= TOK_PAD and past % SEL_BLOCK == 0
    wq = w_in[:, :Q_W].astype(BF16)
    wkv = w_in[:, Q_W:Q_W + 6 * KV_W].astype(BF16)
    wg = w_in[:, Q_W + 6 * KV_W:].astype(BF16)
    wout = w_out.astype(BF16)
    nch_p = t // CMP_STRIDE
    cw = _compress_weights(pe, w1, b1, w2, b2, nch_p)

    cos_p, sin_p = _rope_tables(jnp.arange(t))
    q, gates, cmp_t, sel_t, win_p, kst, vsb, kwt, vwb = _proj_prompt(hp.reshape(b * t, D_MODEL), cos_p, sin_p, wq,
                                                                     wkv, wg, b, layer, n_layers, stacked)
    seq_pages = jnp.zeros((b, t // PAGE_SIZE), jnp.int32)
    kcct, vcc = _compress(cmp_t.reshape(n_layers, b, 2, N_KV, HEAD_DIM, t), layer, seq_pages, cw, paged=False)
    xp_new = _attn_prompt(q.reshape(b, t, Q_W), gates.reshape(b, t, N_GATE), xp, kcct, vcc, kst, vsb, kwt, vwb,
                          _block_bias_matrix(t), _overlap_t(nch_p), wout, g3)
    kv6 = lambda a, lead: a.reshape(lead + (2, N_KV, HEAD_DIM))

    nch_s = past // CMP_STRIDE
    cw_s = cw if nch_s == nch_p else _compress_weights(pe, w1, b1, w2, b2, nch_s)
    cos_s, sin_s = _rope_tables(past + jnp.arange(s_n * n_tok) % n_tok)
    q_s, cmp_s, sel_s, win_s, gates_s = _proj_sample(hs.reshape(s_n * n_tok, D_MODEL), cos_s, sin_s, wq, wkv, wg)
    kcct_s, vcc_s = _compress(cmp_pool_t, layer, page_table, cw_s)
    per_seq = lambda a: a.reshape(s_n, n_tok, a.shape[-1])
    xs_new = _attn_sample(per_seq(q_s), per_seq(gates_s), xs, kcct_s, vcc_s, per_seq(sel_s), per_seq(win_s),
                          win_buf_t, sel_pool_t, layer, page_table, _block_expand_matrix(past), _overlap_t(nch_s),
                          wout, g3, past)
    win_all = jnp.concatenate([win_buf, kv6(win_s, (s_n, n_tok))], axis=1)[:, n_tok:]
    outs_s = (kv6(cmp_s, (s_n, n_tok)), kv6(sel_s, (s_n, n_tok)), win_all)
    return xp_new, xs_new, (cmp_t, sel_t), kv6(win_p, (b, win_p.shape[1])), outs_s


def _conv_layer(hp, xp, hs, xs, state, w_pw1, b_pw1, w_dw, b_dw, ln_g, ln_b, w_pw2, b_pw2, g3):
    s_n, n_tok, _ = xs.shape
    cv = {"w1": w_pw1.astype(BF16), "b1": b_pw1[None], "wdw": w_dw, "bdw": b_dw[None], "lng": ln_g[None],
          "lnb": ln_b[None], "w2": w_pw2.astype(BF16), "b2": b_pw2[None]}
    xp_new, st_p = _conv_prompt(hp, xp, cv, g3)
    to_ts = lambda a: a.reshape(s_n, n_tok, -1).transpose(1, 0, 2).reshape(s_n * n_tok, -1)
    xs_ts, st_t = _conv_sample(to_ts(hs), to_ts(xs), state.transpose(1, 0, 2), cv, g3, n_tok)
    xs_new = xs_ts.reshape(n_tok, s_n, -1).transpose(1, 0, 2)
    return xp_new, xs_new, st_p, st_t.transpose(1, 0, 2)


def kernel(x_prompt, x_sample, cache_cmp_kv, cache_sel_kv, state_win_kv, state_conv, page_table, norm_g, ffn_w_gu,
           ffn_w_down, attn_w_in, attn_w_out, cmp_pe, cmp_w1, cmp_b1, cmp_w2, cmp_b2, conv_w_pw1, conv_b_pw1,
           conv_w_dw, conv_b_dw, conv_ln_g, conv_ln_b, conv_w_pw2, conv_b_pw2):
    b, t, d = x_prompt.shape
    s_n, n_tok, _ = x_sample.shape
    depth = norm_g.shape[0]
    xp = x_prompt.reshape(b * t, d)
    xs = x_sample.reshape(s_n * n_tok, d)
    win_p, conv_p = [], []
    cmp_s, sel_s, win_s, conv_s = [], [], [], []
    cmp_pool_t = _rows_minor(cache_cmp_kv)
    sel_pool_t = _rows_minor(cache_sel_kv)
    win_buf_t = _rows_minor(state_win_kv)
    wgu_all, wdn_all = ffn_w_gu.astype(BF16), ffn_w_down.astype(BF16)
    n_attn = (depth + 1) // 2
    stacked = None
    for i in range(depth):
        g = norm_g[i]
        xp, hp = _ffn(xp, g[0], g[1], g[2], wgu_all, wdn_all, i, 0, True)
        xs, hs = _ffn(xs, g[0], g[1], g[2], wgu_all, wdn_all, i, 0, True)
        j = i // 2
        xp3, xs3 = xp.reshape(b, t, d), xs.reshape(s_n, n_tok, d)
        hp3, hs3 = hp.reshape(b, t, d), hs.reshape(s_n, n_tok, d)
        if i % 2 == 0:
            xp3, xs3, stacked, w_p, o_s = _nsa_layer(hp3, xp3, hs3, xs3, cmp_pool_t, sel_pool_t, win_buf_t,
                                                     state_win_kv[j], j, n_attn, stacked, page_table, attn_w_in[j],
                                                     attn_w_out[j], cmp_pe[j], cmp_w1[j], cmp_b1[j], cmp_w2[j],
                                                     cmp_b2[j], g[3])
            win_p.append(w_p)
            cmp_s.append(o_s[0]); sel_s.append(o_s[1]); win_s.append(o_s[2])
        else:
            xp3, xs3, st_p, st_s = _conv_layer(hp3, xp3, hs3, xs3, state_conv[j], conv_w_pw1[j], conv_b_pw1[j],
                                               conv_w_dw[j], conv_b_dw[j], conv_ln_g[j], conv_ln_b[j], conv_w_pw2[j],
                                               conv_b_pw2[j], g[3])
            conv_p.append(st_p); conv_s.append(st_s)
        xp, xs = xp3.reshape(b * t, d), xs3.reshape(s_n * n_tok, d)
        xp, _ = _ffn(xp, g[4], g[5], g[5], wgu_all, wdn_all, i, 1, False)
        xs, _ = _ffn(xs, g[4], g[5], g[5], wgu_all, wdn_all, i, 1, False)
    rows_major = lambda a: a.reshape(n_attn, b, 2, N_KV, HEAD_DIM, t).transpose(0, 1, 5, 2, 3, 4)
    return (xp.reshape(b, t, d), xs.reshape(s_n, n_tok, d), rows_major(stacked[0]), rows_major(stacked[1]),
            jnp.stack(win_p), jnp.stack(conv_p), jnp.stack(cmp_s), jnp.stack(sel_s), jnp.stack(win_s),
            jnp.stack(conv_s))
```
